```python
import jax, jax.numpy as jnp
from jax import lax
import numpy as np

D_MODEL = 1024
BATCH = 2
SEQ = 8192
DEPTH = 4

GRID_W = 64
CTX_LEN = 256
N_MIXERS = 3
HEAD_DIM = 64
N_HEADS = D_MODEL // HEAD_DIM
A_KV_HEADS = 4
C_KV_HEADS = 2
Q_BLOCK = 128
WINDOW = 128
ROPE_THETA = 10000.0
RWKV_HEAD = 64
RWKV_HEADS = D_MODEL // RWKV_HEAD
DECAY_LORA = 64
ICLR_LORA = 64
GATE_LORA = 128
N_GROUPS = 4
EXPERTS_PER_GROUP = 8
N_EXPERTS = N_GROUPS * EXPERTS_PER_GROUP
TOP_K = 2
D_EXPERT = 768
MOE_BLOCK = 128
N_A_LAYERS = (DEPTH + 2) // 3
N_B_LAYERS = (DEPTH + 1) // 3
N_C_LAYERS = DEPTH // 3
NORM_EPS = 1e-6
LN_X_EPS = 64e-5
NEG_INF = -1e30

kernel_name = 'hybrid_dit_gqa_rwkv7_swa_hmoe'


def rms_norm(x, g):
    xf = x.astype(jnp.float32)
    y = xf * lax.rsqrt(jnp.mean(xf * xf, axis=-1, keepdims=True) + NORM_EPS)
    return (y * g.astype(jnp.float32)).astype(x.dtype)


def modulate(x, g, shift, scale):
    return rms_norm(x, g) * (1 + scale) + shift


def axial_rope_tables(rows):
    row = jnp.repeat(jnp.arange(rows, dtype=jnp.float32), GRID_W)
    col = jnp.tile(jnp.arange(GRID_W, dtype=jnp.float32), rows)
    axis_dim = HEAD_DIM // 2
    inv_freq = ROPE_THETA ** (-jnp.arange(0, axis_dim, 2, dtype=jnp.float32) / axis_dim)
    a_row = row[:, None] * inv_freq
    a_col = col[:, None] * inv_freq
    ang = jnp.concatenate([a_row, a_row, a_col, a_col], axis=-1)[:, None, :]
    return jnp.cos(ang), jnp.sin(ang)


def apply_rope(x, cos, sin):
    xr = x.reshape(*x.shape[:-1], 2, 2, HEAD_DIM // 4)
    rot = jnp.concatenate([-xr[..., 1:, :], xr[..., :1, :]], axis=-2).reshape(x.shape)
    return x * cos.astype(x.dtype) + rot * sin.astype(x.dtype)


def split_qkv(qkv, n_kv):
    b, s, _ = qkv.shape
    qd, kd = N_HEADS * HEAD_DIM, n_kv * HEAD_DIM
    q = qkv[..., :qd].reshape(b, s, N_HEADS, HEAD_DIM)
    k = qkv[..., qd:qd + kd].reshape(b, s, n_kv, HEAD_DIM)
    v = qkv[..., qd + kd:].reshape(b, s, n_kv, HEAD_DIM)
    return q, k, v


def attend(q, keys, values, mask=None, sink=None):
    s = jnp.einsum('bqhgd,bkhd->bhgqk', q, keys).astype(jnp.float32) * (HEAD_DIM ** -0.5)
    if mask is not None:
        s = jnp.where(mask, s, NEG_INF)
    if sink is not None:
        sink_col = jnp.broadcast_to(sink.astype(jnp.float32)[None, :, :, None, None], s.shape[:-1] + (1,))
        p = jax.nn.softmax(jnp.concatenate([s, sink_col], axis=-1), axis=-1)[..., :-1]
    else:
        p = jax.nn.softmax(s, axis=-1)
    return jnp.einsum('bhgqk,bkhd->bqhgd', p.astype(values.dtype), values)


def dense_gqa_mixer(h_lat, h_ctx, w_qkv, w_o, q_gain, k_gain, cos, sin, with_ctx_out):
    b, s, _ = h_lat.shape
    n_ctx = h_ctx.shape[1]
    grp = N_HEADS // A_KV_HEADS
    q_l, k_l, v_l = split_qkv(h_lat @ w_qkv, A_KV_HEADS)
    q_c, k_c, v_c = split_qkv(h_ctx @ w_qkv, A_KV_HEADS)
    q_l = apply_rope(rms_norm(q_l, q_gain), cos, sin)
    k_l = apply_rope(rms_norm(k_l, k_gain), cos, sin)
    q_c, k_c = rms_norm(q_c, q_gain), rms_norm(k_c, k_gain)
    k_all = jnp.concatenate([k_c, k_l], axis=1)
    v_all = jnp.concatenate([v_c, v_l], axis=1)
    q_blocks = q_l.reshape(b, s // Q_BLOCK, Q_BLOCK, A_KV_HEADS, grp, HEAD_DIM).transpose(1, 0, 2, 3, 4, 5)
    o_l = lax.map(lambda qb: attend(qb, k_all, v_all), q_blocks)
    y_l = o_l.transpose(1, 0, 2, 3, 4, 5).reshape(b, s, D_MODEL) @ w_o
    y_c = None
    if with_ctx_out:
        o_c = attend(q_c.reshape(b, n_ctx, A_KV_HEADS, grp, HEAD_DIM), k_c, v_c)
        y_c = o_c.reshape(b, n_ctx, D_MODEL) @ w_o
    return y_l, y_c


def window_gqa_mixer(h_lat, h_ctx, w_qkv, w_o, sink, cos, sin, with_ctx_out):
    b, s, _ = h_lat.shape
    n_ctx = h_ctx.shape[1]
    grp = N_HEADS // C_KV_HEADS
    span = Q_BLOCK + 2 * WINDOW
    q_l, k_l, v_l = split_qkv(h_lat @ w_qkv, C_KV_HEADS)
    q_c, k_c, v_c = split_qkv(h_ctx @ w_qkv, C_KV_HEADS)
    q_l = apply_rope(q_l, cos, sin).reshape(b, s, C_KV_HEADS, grp, HEAD_DIM)
    k_l = apply_rope(k_l, cos, sin)
    pad = ((0, 0), (WINDOW, WINDOW), (0, 0), (0, 0))
    k_pad, v_pad = jnp.pad(k_l, pad), jnp.pad(v_l, pad)
    sink = sink.reshape(C_KV_HEADS, grp)
    ctx_mask = jnp.ones((Q_BLOCK, n_ctx), dtype=bool)

    def band_block(blk):
        start = blk * Q_BLOCK
        qb = lax.dynamic_slice_in_dim(q_l, start, Q_BLOCK, axis=1)
        kb = lax.dynamic_slice_in_dim(k_pad, start, span, axis=1)
        vb = lax.dynamic_slice_in_dim(v_pad, start, span, axis=1)
        q_pos = start + jnp.arange(Q_BLOCK)
        k_pos = start - WINDOW + jnp.arange(span)
        band = (jnp.abs(q_pos[:, None] - k_pos[None, :]) <= WINDOW) & (k_pos >= 0)[None, :] & (k_pos < s)[None, :]
        keys = jnp.concatenate([k_c, kb], axis=1)
        vals = jnp.concatenate([v_c, vb], axis=1)
        return attend(qb, keys, vals, jnp.concatenate([ctx_mask, band], axis=1), sink)

    o_l = lax.map(band_block, jnp.arange(s // Q_BLOCK))
    y_l = o_l.transpose(1, 0, 2, 3, 4, 5).reshape(b, s, D_MODEL) @ w_o
    y_c = None
    if with_ctx_out:
        o_c = attend(q_c.reshape(b, n_ctx, C_KV_HEADS, grp, HEAD_DIM), k_c, v_c, sink=sink)
        y_c = o_c.reshape(b, n_ctx, D_MODEL) @ w_o
    return y_l, y_c


def token_shift_centred(x):
    prev = jnp.pad(x[:, :-1], ((0, 0), (1, 0), (0, 0)))
    nxt = jnp.pad(x[:, 1:], ((0, 0), (0, 1), (0, 0)))
    return 0.5 * (prev + nxt)


def rwkv_heads(t):
    return t.reshape(*t.shape[:-1], RWKV_HEADS, RWKV_HEAD)


def rwkv7_features(h, mu, w_r, w_k, w_v, w0, w1, w2, a0, a1, a2, g1, g2, k_k, k_a):
    xx = token_shift_centred(h) - h
    xr, xw, xk, xv, xa, xg = [h + xx * mu[i] for i in range(6)]
    r, k, v = xr @ w_r, xk @ w_k, xv @ w_v
    g = jax.nn.sigmoid(xg @ g1) @ g2
    w_log = w0[:, None, None, :] + jnp.einsum('nbsr,nrd->nbsd', jnp.tanh(jnp.einsum('bsd,ndr->nbsr', xw, w1)), w2)
    decay = jnp.exp(-jnp.exp(-jax.nn.softplus(-w_log.astype(jnp.float32)) - 0.5))
    iclr = jax.nn.sigmoid((a0[:, None, None, :] + jnp.einsum('nbsr,nrd->nbsd', jnp.einsum('bsd,ndr->nbsr', xa, a1), a2)).astype(jnp.float32))
    kk = rwkv_heads((k * k_k).astype(jnp.float32))
    kk = (kk / jnp.maximum(jnp.linalg.norm(kk, axis=-1, keepdims=True), 1e-12)).reshape(k.shape)
    k_dir = k[None].astype(jnp.float32) * (1 + (iclr - 1) * k_a.astype(jnp.float32))
    return r, v, g, decay, iclr, kk, k_dir


def rwkv7_scan(state0, r, decay, k, v, kk, iclr, reverse):
    xs = tuple(jnp.moveaxis(rwkv_heads(t.astype(jnp.float32)), 1, 0) for t in (r, decay, k, v, kk, iclr))

    def step(st, inp):
        r_t, w_t, k_t, v_t, kk_t, a_t = inp
        sa = jnp.einsum('bhvk,bhk->bhv', st, -kk_t)
        st = st * w_t[:, :, None, :] + sa[..., None] * (kk_t * a_t)[:, :, None, :] + v_t[..., None] * k_t[:, :, None, :]
        return st, jnp.einsum('bhvk,bhk->bhv', st, r_t)

    s_final, ys = lax.scan(step, state0, xs, reverse=reverse)
    y = jnp.moveaxis(ys, 0, 1)
    return s_final, y.reshape(*y.shape[:2], D_MODEL)


def rwkv7_bidir(feats, state_fwd, state_bwd):
    r, v, _, decay, iclr, kk, k_dir = feats
    s_f, y_f = rwkv7_scan(state_fwd, r, decay[0], k_dir[0], v, kk, iclr[0], reverse=False)
    s_b, y_b = rwkv7_scan(state_bwd, r, decay[1], k_dir[1], v, kk, iclr[1], reverse=True)
    return s_f, s_b, y_f + y_b


def rwkv7_output(y, feats, r_k, ln_w, ln_b, w_o):
    r, v, g, _, _, _, k_dir = feats
    yh = rwkv_heads(y)
    mean = jnp.mean(yh, axis=-1, keepdims=True)
    var = jnp.mean(jnp.square(yh - mean), axis=-1, keepdims=True)
    yn = ((yh - mean) * lax.rsqrt(var + LN_X_EPS)).reshape(y.shape) * ln_w + ln_b
    bonus = jnp.sum(rwkv_heads(r.astype(jnp.float32) * (k_dir[0] + k_dir[1])) * r_k, axis=-1, keepdims=True) * rwkv_heads(v.astype(jnp.float32))
    out = (yn + bonus.reshape(y.shape)).astype(g.dtype) * g
    return out @ w_o


def rwkv7_mixer(h_lat, h_ctx, p, with_ctx_out):
    (mu, w_r, w_k, w_v, w_o, w0, w1, w2, a0, a1, a2, g1, g2, k_k, k_a, r_k, ln_w, ln_b) = p
    f_c = rwkv7_features(h_ctx, mu, w_r, w_k, w_v, w0, w1, w2, a0, a1, a2, g1, g2, k_k, k_a)
    f_l = rwkv7_features(h_lat, mu, w_r, w_k, w_v, w0, w1, w2, a0, a1, a2, g1, g2, k_k, k_a)
    zero = jnp.zeros((h_lat.shape[0], RWKV_HEADS, RWKV_HEAD, RWKV_HEAD), jnp.float32)
    s_f, s_b, y_c = rwkv7_bidir(f_c, zero, zero)
    _, _, y_l = rwkv7_bidir(f_l, s_f, s_b)
    y_lat = rwkv7_output(y_l, f_l, r_k, ln_w, ln_b, w_o).astype(h_lat.dtype)
    y_ctx = rwkv7_output(y_c, f_c, r_k, ln_w, ln_b, w_o).astype(h_ctx.dtype) if with_ctx_out else None
    return y_lat, y_ctx


def hier_moe(h, w_group, b_group, w_expert, b_expert, w1, w3, w2):
    n_tok, d = h.shape
    g_logit = (h @ w_group).astype(jnp.float32) + b_group.astype(jnp.float32)
    g_idx = jnp.argmax(g_logit, axis=-1)
    g_w = jnp.take_along_axis(jax.nn.softmax(g_logit, axis=-1), g_idx[:, None], axis=-1)
    e_logit = ((h @ w_expert).astype(jnp.float32) + b_expert.astype(jnp.float32)).reshape(n_tok, N_GROUPS, EXPERTS_PER_GROUP)
    e_logit = jnp.take_along_axis(e_logit, g_idx[:, None, None], axis=1)[:, 0]
    top_p, top_i = lax.top_k(jax.nn.softmax(e_logit, axis=-1), TOP_K)
    weight = (g_w * top_p / jnp.sum(top_p, axis=-1, keepdims=True)).reshape(-1)
    expert = (g_idx[:, None] * EXPERTS_PER_GROUP + top_i).reshape(-1)
    token = jnp.repeat(jnp.arange(n_tok, dtype=jnp.int32), TOP_K)
    n_assign = n_tok * TOP_K
    n_blocks = -(-n_assign // MOE_BLOCK) + N_EXPERTS
    order = jnp.argsort(expert)
    e_sorted = expert[order]
    counts = jax.ops.segment_sum(jnp.ones_like(expert), expert, num_segments=N_EXPERTS)
    start = jnp.cumsum(counts) - counts
    padded = (counts + MOE_BLOCK - 1) // MOE_BLOCK * MOE_BLOCK
    p_end = jnp.cumsum(padded)
    dest = (p_end - padded)[e_sorted] + jnp.arange(n_assign) - start[e_sorted]
    tok_pad = jnp.full((n_blocks * MOE_BLOCK,), n_tok, jnp.int32).at[dest].set(token[order])
    w_pad = jnp.zeros((n_blocks * MOE_BLOCK,), jnp.float32).at[dest].set(weight[order])
    blk_expert = jnp.minimum(jnp.searchsorted(p_end, jnp.arange(n_blocks) * MOE_BLOCK, side='right'), N_EXPERTS - 1)
    h_pad = jnp.concatenate([h, jnp.zeros((1, d), h.dtype)], axis=0)

    def expert_block(args):
        tok_b, w_b, e = args
        xb = h_pad[tok_b]
        hid = jax.nn.silu(xb @ w1[e]) * (xb @ w3[e])
        return (hid @ w2[e]) * w_b[:, None].astype(h.dtype)

    ys = lax.map(expert_block, (tok_pad.reshape(n_blocks, MOE_BLOCK), w_pad.reshape(n_blocks, MOE_BLOCK), blk_expert))
    out = jnp.zeros_like(h_pad).at[tok_pad].add(ys.reshape(-1, d))
    return out[:n_tok]


def setup_inputs(seed: int = 0) -> dict:
    key = jax.random.key(seed)
    ks = iter(jax.random.split(key, 64))

    def nrm(shape, scale):
        return jax.random.normal(next(ks), shape, jnp.float32) * scale

    def gain(shape):
        return 1.0 + nrm(shape, 0.05)

    d = D_MODEL
    qkv_a = (N_HEADS + 2 * A_KV_HEADS) * HEAD_DIM
    qkv_c = (N_HEADS + 2 * C_KV_HEADS) * HEAD_DIM
    decay_ramp = jnp.linspace(-6.5, -1.5, d, dtype=jnp.float32)
    return {
        'x': nrm((BATCH, SEQ, d), 1.0),
        'c': nrm((BATCH, d), 1.0),
        'ctx': nrm((BATCH, CTX_LEN, d), 1.0),
        'c_ctx': nrm((d,), 1.0),
        'mod_w': nrm((DEPTH, d, 6 * d), 0.5 * d ** -0.5),
        'mod_b': nrm((DEPTH, 6 * d), 0.02),
        'norm1_g': gain((DEPTH, d)),
        'norm2_g': gain((DEPTH, d)),
        'a_w_qkv': nrm((N_A_LAYERS, d, qkv_a), d ** -0.5),
        'a_w_o': nrm((N_A_LAYERS, d, d), d ** -0.5),
        'a_q_gain': gain((N_A_LAYERS, HEAD_DIM)),
        'a_k_gain': gain((N_A_LAYERS, HEAD_DIM)),
        'b_mu': jax.random.uniform(next(ks), (N_B_LAYERS, 6, d), jnp.float32),
        'b_w_r': nrm((N_B_LAYERS, d, d), d ** -0.5),
        'b_w_k': nrm((N_B_LAYERS, d, d), d ** -0.5),
        'b_w_v': nrm((N_B_LAYERS, d, d), d ** -0.5),
        'b_w_o': nrm((N_B_LAYERS, d, d), d ** -0.5),
        'b_decay_w0': decay_ramp + nrm((N_B_LAYERS, 2, d), 0.1),
        'b_decay_w1': nrm((N_B_LAYERS, 2, d, DECAY_LORA), 0.5 * d ** -0.5),
        'b_decay_w2': nrm((N_B_LAYERS, 2, DECAY_LORA, d), 0.5 * DECAY_LORA ** -0.5),
        'b_iclr_a0': nrm((N_B_LAYERS, 2, d), 0.3),
        'b_iclr_a1': nrm((N_B_LAYERS, 2, d, ICLR_LORA), d ** -0.5),
        'b_iclr_a2': nrm((N_B_LAYERS, 2, ICLR_LORA, d), 0.5 * ICLR_LORA ** -0.5),
        'b_gate_g1': nrm((N_B_LAYERS, d, GATE_LORA), d ** -0.5),
        'b_gate_g2': nrm((N_B_LAYERS, GATE_LORA, d), GATE_LORA ** -0.5),
        'b_k_k': 0.85 + nrm((N_B_LAYERS, d), 0.05),
        'b_k_a': gain((N_B_LAYERS, d)),
        'b_r_k': nrm((N_B_LAYERS, RWKV_HEADS, RWKV_HEAD), 0.1),
        'b_ln_w': gain((N_B_LAYERS, d)),
        'b_ln_b': nrm((N_B_LAYERS, d), 0.02),
        'c_w_qkv': nrm((N_C_LAYERS, d, qkv_c), d ** -0.5),
        'c_w_o': nrm((N_C_LAYERS, d, d), d ** -0.5),
        'c_sink': nrm((N_C_LAYERS, N_HEADS), 0.5),
        'moe_w_group': nrm((DEPTH, d, N_GROUPS), d ** -0.5),
        'moe_b_group': nrm((DEPTH, N_GROUPS), 0.01),
        'moe_w_expert': nrm((DEPTH, d, N_EXPERTS), d ** -0.5),
        'moe_b_expert': nrm((DEPTH, N_EXPERTS), 0.01),
        'moe_w1': nrm((DEPTH, N_EXPERTS, d, D_EXPERT), d ** -0.5),
        'moe_w3': nrm((DEPTH, N_EXPERTS, d, D_EXPERT), d ** -0.5),
        'moe_w2': nrm((DEPTH, N_EXPERTS, D_EXPERT, d), D_EXPERT ** -0.5),
        'final_g': gain((d,)),
    }


def reference(x, c, ctx, c_ctx, mod_w, mod_b, norm1_g, norm2_g, a_w_qkv, a_w_o, a_q_gain, a_k_gain,
              b_mu, b_w_r, b_w_k, b_w_v, b_w_o, b_decay_w0, b_decay_w1, b_decay_w2, b_iclr_a0, b_iclr_a1,
              b_iclr_a2, b_gate_g1, b_gate_g2, b_k_k, b_k_a, b_r_k, b_ln_w, b_ln_b, c_w_qkv, c_w_o, c_sink,
              moe_w_group, moe_b_group, moe_w_expert, moe_b_expert, moe_w1, moe_w3, moe_w2, final_g):
    b, s, d = x.shape
    n_ctx = ctx.shape[1]
    rows = s // GRID_W
    cos, sin = axial_rope_tables(rows)
    cond_lat = jax.nn.silu(c)
    cond_ctx = jax.nn.silu(c_ctx)
    x_lat, x_ctx = x, ctx
    for l in range(DEPTH):
        last = l == DEPTH - 1
        m_lat = jnp.split((cond_lat @ mod_w[l] + mod_b[l])[:, None, :], 6, axis=-1)
        m_ctx = jnp.split(cond_ctx @ mod_w[l] + mod_b[l], 6, axis=-1)
        h_lat = modulate(x_lat, norm1_g[l], m_lat[0], m_lat[1])
        h_ctx = modulate(x_ctx, norm1_g[l], m_ctx[0], m_ctx[1])
        kind, j = l % N_MIXERS, l // N_MIXERS
        if kind == 0:
            y_lat, y_ctx = dense_gqa_mixer(h_lat, h_ctx, a_w_qkv[j], a_w_o[j], a_q_gain[j], a_k_gain[j], cos, sin, not last)
        elif kind == 1:
            params = (b_mu[j], b_w_r[j], b_w_k[j], b_w_v[j], b_w_o[j], b_decay_w0[j], b_decay_w1[j], b_decay_w2[j],
                      b_iclr_a0[j], b_iclr_a1[j], b_iclr_a2[j], b_gate_g1[j], b_gate_g2[j], b_k_k[j], b_k_a[j],
                      b_r_k[j], b_ln_w[j], b_ln_b[j])
            y_lat, y_ctx = rwkv7_mixer(h_lat, h_ctx, params, not last)
        else:
            y_lat, y_ctx = window_gqa_mixer(h_lat, h_ctx, c_w_qkv[j], c_w_o[j], c_sink[j], cos, sin, not last)
        x_lat = x_lat + m_lat[2] * y_lat
        h2_lat = modulate(x_lat, norm2_g[l], m_lat[3], m_lat[4])
        moe_p = (moe_w_group[l], moe_b_group[l], moe_w_expert[l], moe_b_expert[l], moe_w1[l], moe_w3[l], moe_w2[l])
        if last:
            f_lat = hier_moe(h2_lat.reshape(-1, d), *moe_p).reshape(b, s, d)
            x_lat = x_lat + m_lat[5] * f_lat
        else:
            x_ctx = x_ctx + m_ctx[2] * y_ctx
            h2_ctx = modulate(x_ctx, norm2_g[l], m_ctx[3], m_ctx[4])
            f_all = hier_moe(jnp.concatenate([h2_ctx.reshape(-1, d), h2_lat.reshape(-1, d)], axis=0), *moe_p)
            x_ctx = x_ctx + m_ctx[5] * f_all[:b * n_ctx].reshape(b, n_ctx, d)
            x_lat = x_lat + m_lat[5] * f_all[b * n_ctx:].reshape(b, s, d)
    return rms_norm(x_lat, final_g)
```

```python
import functools

import jax
import jax.numpy as jnp
from jax import lax
from jax.experimental import pallas as pl
from jax.experimental.pallas import tpu as pltpu

F32 = jnp.float32
BF16 = jnp.bfloat16
HI = lax.Precision.HIGHEST

HEAD = 64
LANES = 128
TM = 256
ATT_TQ = 128
ATT_TK = 256
WINDOW = 128
SCAN_CHUNK = 64
MOE_ROWS = 256
N_GROUPS = 4
EXPERTS_PER_GROUP = 8
N_EXPERTS = N_GROUPS * EXPERTS_PER_GROUP
ROPE_THETA = 10000.0
NEG_INF = -1e30
NORM_EPS = 1e-6
LN_X_EPS = 64e-5
VMEM_LIMIT = 56 * 1024 * 1024


def _params(n_axes):
    return pltpu.CompilerParams(dimension_semantics=("arbitrary",) * n_axes, vmem_limit_bytes=VMEM_LIMIT)


def _dot(a, b, precision=None):
    return jnp.dot(a, b, preferred_element_type=F32, precision=precision)


def _dot_nt(a, b, precision=None):
    return lax.dot_general(a, b, (((1,), (1,)), ((), ())), preferred_element_type=F32, precision=precision)


def _dot_tn(a, b, precision=None):
    return lax.dot_general(a, b, (((0,), (0,)), ((), ())), preferred_element_type=F32, precision=precision)


def _sigmoid(z):
    return 1.0 / (1.0 + jnp.exp(-z))


def _modulate(x, g, shift, scale):
    y = x * lax.rsqrt(jnp.mean(x * x, axis=-1, keepdims=True) + NORM_EPS)
    return (y * g) * (1.0 + scale) + shift


def _seg_apply(t, seg):
    n = t.shape[1] // LANES
    return jnp.concatenate([_dot(t[:, i * LANES:(i + 1) * LANES], seg, HI) for i in range(n)], axis=1)


def _mod_kernel(c_ref, w_ref, b_ref, o_ref):
    cond = c_ref[...]
    o_ref[0] = _dot(cond * _sigmoid(cond), w_ref[0], HI) + b_ref[0]


def _mod_vectors(cond, mod_w, mod_b):
    depth, d, n = mod_w.shape
    tn = 1536
    return pl.pallas_call(
        _mod_kernel,
        grid=(depth, n // tn),
        in_specs=[pl.BlockSpec(cond.shape, lambda l, j: (0, 0)),
                  pl.BlockSpec((1, d, tn), lambda l, j: (l, 0, j)),
                  pl.BlockSpec((1, 1, tn), lambda l, j: (l, 0, j))],
        out_specs=pl.BlockSpec((1, cond.shape[0], tn), lambda l, j: (l, 0, j)),
        out_shape=jax.ShapeDtypeStruct((depth, cond.shape[0], n), F32),
        compiler_params=_params(2),
        name="mod_vectors",
    )(cond, mod_w, mod_b.reshape(depth, 1, n))


class _Layout:
    def __init__(self, n_batch, n_ctx, seq):
        assert n_ctx == TM and seq % TM == 0
        self.n_batch, self.n_ctx, self.seq = n_batch, n_ctx, seq
        self.per_batch = n_ctx + seq
        self.blocks_per_batch = self.per_batch // TM
        self.n_blocks = n_batch * self.blocks_per_batch
        self.t_all = n_batch * self.per_batch

    def mod_index(self, j):
        jb = j % self.blocks_per_batch
        return jnp.where(jb == 0, self.n_batch, j // self.blocks_per_batch)

    def mod_spec(self, d):
        return pl.BlockSpec((1, 6, d), lambda j: (self.mod_index(j), 0, 0))

    def row_spec(self, width):
        return pl.BlockSpec((TM, width), lambda j: (j, 0))

    def pos_spec(self, width):
        return pl.BlockSpec((TM, width), lambda j: (j % self.blocks_per_batch, 0))


def _full(shape):
    return pl.BlockSpec(shape, lambda *_: (0,) * len(shape))


def _qkv_kernel(x_ref, mod_ref, g_ref, w_ref, cos_ref, sina_ref, sinb_ref, qg_ref, kg_ref, seg_ref,
                q_ref, k_ref, v_ref, *, n_q, n_kv, qk_norm):
    h = _modulate(x_ref[...], g_ref[...], mod_ref[0, 0:1], mod_ref[0, 1:2])
    y = _dot(h.astype(BF16), w_ref[...])
    cos, sina, sinb = cos_ref[...], sina_ref[...], sinb_ref[...]
    seg = seg_ref[...]

    def head_pair(t, gain):
        if qk_norm:
            t = t * lax.rsqrt(_dot(t * t, seg, HI) + NORM_EPS) * gain
        return t * cos + pltpu.roll(t, LANES - 16, 1) * sina + pltpu.roll(t, 16, 1) * sinb

    for i in range(n_q // LANES):
        t = head_pair(y[:, i * LANES:(i + 1) * LANES], qg_ref[...])
        q_ref[:, i * LANES:(i + 1) * LANES] = (t * (HEAD ** -0.5)).astype(BF16)
    for i in range(n_kv // LANES):
        t = head_pair(y[:, n_q + i * LANES:n_q + (i + 1) * LANES], kg_ref[...])
        k_ref[:, i * LANES:(i + 1) * LANES] = t.astype(BF16)
    v_ref[...] = y[:, n_q + n_kv:].astype(BF16)


def _qkv_project(lay, x, mod, g, w, rope, q_gain, k_gain, seg_mean, *, n_kv_heads, qk_norm):
    d = x.shape[1]
    n_q, n_kv = d, n_kv_heads * HEAD
    cos, sina, sinb = rope
    pair = lambda v: jnp.tile(v.reshape(1, HEAD), (1, LANES // HEAD))
    return pl.pallas_call(
        functools.partial(_qkv_kernel, n_q=n_q, n_kv=n_kv, qk_norm=qk_norm),
        grid=(lay.n_blocks,),
        in_specs=[lay.row_spec(d), lay.mod_spec(d), _full((1, d)), _full(w.shape),
                  lay.pos_spec(LANES), lay.pos_spec(LANES), lay.pos_spec(LANES),
                  _full((1, LANES)), _full((1, LANES)), _full((LANES, LANES))],
        out_specs=[lay.row_spec(n_q), lay.row_spec(n_kv), lay.row_spec(n_kv)],
        out_shape=[jax.ShapeDtypeStruct((lay.t_all, n_q), BF16),
                   jax.ShapeDtypeStruct((lay.t_all, n_kv), BF16),
                   jax.ShapeDtypeStruct((lay.t_all, n_kv), BF16)],
        compiler_params=_params(1),
        name="qkv_project",
    )(x, mod, g.reshape(1, d), w.astype(BF16), cos, sina, sinb, pair(q_gain), pair(k_gain), seg_mean)


def _rope_tables(lay):
    rows = lay.seq // 64
    grid_w = 64
    row = jnp.repeat(jnp.arange(rows, dtype=F32), grid_w)
    col = jnp.tile(jnp.arange(grid_w, dtype=F32), rows)
    axis_dim = HEAD // 2
    inv_freq = ROPE_THETA ** (-jnp.arange(0, axis_dim, 2, dtype=F32) / axis_dim)
    a_row = row[:, None] * inv_freq
    a_col = col[:, None] * inv_freq
    ang = jnp.concatenate([a_row, a_row, a_col, a_col], axis=-1)
    cos = jnp.concatenate([jnp.ones((lay.n_ctx, HEAD), F32), jnp.cos(ang)], axis=0)
    sin = jnp.concatenate([jnp.zeros((lay.n_ctx, HEAD), F32), jnp.sin(ang)], axis=0)
    first = (jnp.arange(HEAD) % 32) < 16
    sina = jnp.where(first, -sin, 0.0)
    sinb = jnp.where(first, 0.0, sin)
    wide = lambda t: jnp.tile(t, (1, LANES // HEAD))
    return wide(cos), wide(sina), wide(sinb)


def _attn_kernel(sink_ref, q_ref, k_ref, v_ref, o_ref, *, n_kv, grp, window, n_ctx, seq):
    tq = ATT_TQ
    j = pl.program_id(1)
    rows = grp * tq
    ctx_blocks = n_ctx // tq

    def online(carry, s, vt):
        m, l, acc = carry
        m_new = jnp.maximum(m, jnp.max(s, axis=1, keepdims=True))
        p = jnp.exp(s - m_new)
        alpha = jnp.exp(m - m_new)
        l = alpha * l + jnp.sum(p, axis=1, keepdims=True)
        acc = alpha * acc + _dot(p.astype(BF16), vt)
        return m_new, l, acc

    for g in range(n_kv):
        ks = slice(g * HEAD, (g + 1) * HEAD)
        qg = jnp.concatenate([q_ref[:, (g * grp + u) * HEAD:(g * grp + u + 1) * HEAD] for u in range(grp)], axis=0)
        acc0 = jnp.zeros((rows, HEAD), F32)
        if window:
            m0 = jnp.concatenate([jnp.full((tq, 1), sink_ref[g * grp + u], F32) for u in range(grp)], axis=0)
            carry = (m0, jnp.ones((rows, 1), F32), acc0)
            tk = WINDOW
            rr = lax.broadcasted_iota(jnp.int32, (rows, tk), 0) & (tq - 1)
            cc = lax.broadcasted_iota(jnp.int32, (rows, tk), 1)
            j_lat = j - ctx_blocks
            n_lat = seq // tq
            never = 4 * tk
            for cb in range(n_ctx // tk):
                s = _dot_nt(qg, k_ref[cb * tk:(cb + 1) * tk, ks])
                carry = online(carry, s, v_ref[cb * tk:(cb + 1) * tk, ks])
            for off in (-1, 0, 1):
                blk = j_lat + off
                valid = jnp.logical_and(j_lat >= 0, jnp.logical_and(blk >= 0, blk < n_lat))
                start = pl.multiple_of(n_ctx + jnp.clip(blk, 0, n_lat - 1) * tk, tk)
                s = _dot_nt(qg, k_ref[pl.ds(start, tk), ks])
                bar = jnp.where(valid, 0, never)
                if off == -1:
                    keep = cc >= rr + bar
                elif off == 0:
                    keep = cc >= bar
                else:
                    keep = cc + bar <= rr
                carry = online(carry, jnp.where(keep, s, NEG_INF), v_ref[pl.ds(start, tk), ks])
        else:
            tk = ATT_TK
            carry = (jnp.full((rows, 1), NEG_INF, F32), jnp.zeros((rows, 1), F32), acc0)
            n_tiles = jnp.where(j < ctx_blocks, n_ctx // tk, (n_ctx + seq) // tk)

            def body(t, c):
                start = pl.multiple_of(t * tk, tk)
                s = _dot_nt(qg, k_ref[pl.ds(start, tk), ks])
                return online(c, s, v_ref[pl.ds(start, tk), ks])

            carry = lax.fori_loop(0, n_tiles, body, carry)
        _, l, acc = carry
        out = acc / l
        for u in range(grp):
            hd = g * grp + u
            o_ref[:, hd * HEAD:(hd + 1) * HEAD] = out[u * tq:(u + 1) * tq].astype(BF16)


def _attention(lay, q, k, v, sink, *, n_kv_heads, window):
    d = q.shape[1]
    n_kv = n_kv_heads * HEAD
    per_q = lay.per_batch // ATT_TQ
    kv_spec = pl.BlockSpec((lay.per_batch, n_kv), lambda b, j, *_: (b, 0))
    q_spec = pl.BlockSpec((ATT_TQ, d), lambda b, j, *_: (b * per_q + j, 0))
    return pl.pallas_call(
        functools.partial(_attn_kernel, n_kv=n_kv_heads, grp=d // HEAD // n_kv_heads, window=window,
                          n_ctx=lay.n_ctx, seq=lay.seq),
        grid_spec=pltpu.PrefetchScalarGridSpec(
            num_scalar_prefetch=1, grid=(lay.n_batch, per_q),
            in_specs=[q_spec, kv_spec, kv_spec], out_specs=q_spec),
        out_shape=jax.ShapeDtypeStruct((lay.t_all, d), BF16),
        compiler_params=_params(2),
        name="window_attention" if window else "dense_attention",
    )(sink, q, k, v)


def _route(logits):
    lane = lax.broadcasted_iota(jnp.int32, logits.shape, 1)
    far = 4 * LANES
    is_g = lane < N_GROUPS
    gmax = jnp.max(jnp.where(is_g, logits, NEG_INF), axis=1, keepdims=True)
    gidx = jnp.min(jnp.where(jnp.logical_and(is_g, logits == gmax), lane, far), axis=1, keepdims=True)
    gsum = jnp.sum(jnp.where(is_g, jnp.exp(logits - gmax), 0.0), axis=1, keepdims=True)
    g_w = 1.0 / gsum
    is_e = jnp.logical_and(lane >= N_GROUPS, ((lane - N_GROUPS) >> 3) == gidx)
    emax = jnp.max(jnp.where(is_e, logits, NEG_INF), axis=1, keepdims=True)
    ee = jnp.where(is_e, jnp.exp(logits - emax), 0.0)
    p = ee / jnp.sum(ee, axis=1, keepdims=True)
    p_in = jnp.where(is_e, p, -1.0)
    p1 = jnp.max(p_in, axis=1, keepdims=True)
    i1 = jnp.min(jnp.where(p_in == p1, lane, far), axis=1, keepdims=True)
    p_rest = jnp.where(lane == i1, -1.0, p_in)
    p2 = jnp.max(p_rest, axis=1, keepdims=True)
    i2 = jnp.min(jnp.where(p_rest == p2, lane, far), axis=1, keepdims=True)
    tot = p1 + p2
    w1 = g_w * p1 / tot
    w2 = g_w * p2 / tot
    e1 = (i1 - N_GROUPS).astype(F32)
    e2 = (i2 - N_GROUPS).astype(F32)
    return jnp.where(lane == 0, e1, jnp.where(lane == 1, e2, jnp.where(lane == 2, w1, jnp.where(lane == 3, w2, 0.0))))


def _post_kernel(o_ref, wo_ref, x_ref, mod_ref, g2_ref, wr_ref, br_ref, xn_ref, h2_ref, route_ref):
    y = _dot(o_ref[...], wo_ref[...])
    xn = x_ref[...] + mod_ref[0, 2:3] * y
    xn_ref[...] = xn
    h2 = _modulate(xn, g2_ref[...], mod_ref[0, 3:4], mod_ref[0, 4:5])
    h2_ref[...] = h2
    route_ref[...] = _route(_dot(h2, wr_ref[...], HI) + br_ref[...])


def _post_mixer(lay, o, w_o, x, mod, g2, w_route, b_route):
    d = x.shape[1]
    return pl.pallas_call(
        _post_kernel,
        grid=(lay.n_blocks,),
        in_specs=[lay.row_spec(d), _full((d, d)), lay.row_spec(d), lay.mod_spec(d), _full((1, d)),
                  _full((d, LANES)), _full((1, LANES))],
        out_specs=[lay.row_spec(d), lay.row_spec(d), lay.row_spec(LANES)],
        out_shape=[jax.ShapeDtypeStruct((lay.t_all, d), F32), jax.ShapeDtypeStruct((lay.t_all, d), F32),
                   jax.ShapeDtypeStruct((lay.t_all, LANES), F32)],
        compiler_params=_params(1),
        name="post_mixer",
    )(o, w_o.astype(BF16), x, mod, g2.reshape(1, d), w_route, b_route)


def _expert_kernel(blk_ref, tok_ref, h_hbm, w1_ref, w3_ref, w2_ref, wgt_ref, out_ref, xbuf, sem):
    del blk_ref

    def row_copy(r):
        return pltpu.make_async_copy(h_hbm.at[pl.ds(tok_ref[0, 0, r], 1)], xbuf.at[pl.ds(r, 1)], sem)

    def start(r, carry):
        row_copy(r).start()
        return carry

    def wait(r, carry):
        row_copy(r).wait()
        return carry

    lax.fori_loop(0, MOE_ROWS, start, 0)
    lax.fori_loop(0, MOE_ROWS, wait, 0)
    xb = xbuf[...].astype(BF16)
    a = _dot(xb, w1_ref[0].astype(BF16))
    hid = (a * _sigmoid(a)) * _dot(xb, w3_ref[0].astype(BF16))
    out_ref[...] = _dot(hid.astype(BF16), w2_ref[0].astype(BF16)) * wgt_ref[...]


def _expert_blocks(h2, tok_pad, w_pad, blk_expert, w1, w3, w2):
    n_rows = tok_pad.shape[0]
    n_blocks = n_rows // MOE_ROWS
    d, de = w1.shape[1], w1.shape[2]
    return pl.pallas_call(
        _expert_kernel,
        grid_spec=pltpu.PrefetchScalarGridSpec(
            num_scalar_prefetch=1, grid=(n_blocks,),
            in_specs=[pl.BlockSpec((1, 1, MOE_ROWS), lambda i, blk: (i, 0, 0), memory_space=pltpu.SMEM),
                      pl.BlockSpec(memory_space=pl.ANY),
                      pl.BlockSpec((1, d, de), lambda i, blk: (blk[i], 0, 0)),
                      pl.BlockSpec((1, d, de), lambda i, blk: (blk[i], 0, 0)),
                      pl.BlockSpec((1, de, d), lambda i, blk: (blk[i], 0, 0)),
                      pl.BlockSpec((MOE_ROWS, 1), lambda i, blk: (i, 0))],
            out_specs=pl.BlockSpec((MOE_ROWS, d), lambda i, blk: (i, 0)),
            scratch_shapes=[pltpu.VMEM((MOE_ROWS, d), F32), pltpu.SemaphoreType.DMA(())]),
        out_shape=jax.ShapeDtypeStruct((n_rows, d), F32),
        compiler_params=_params(1),
        name="expert_blocks",
    )(blk_expert, tok_pad.reshape(n_blocks, 1, MOE_ROWS), h2, w1, w3, w2, w_pad.reshape(n_rows, 1))


def _combine_kernel(pos_ref, ys_hbm, x_ref, mod_ref, o_ref, buf, sem):
    def row_copy(r, k):
        return pltpu.make_async_copy(ys_hbm.at[pl.ds(pos_ref[0, k, r], 1)], buf.at[k, pl.ds(r, 1)], sem)

    def start(r, carry):
        row_copy(r, 0).start()
        row_copy(r, 1).start()
        return carry

    def wait(r, carry):
        row_copy(r, 0).wait()
        row_copy(r, 1).wait()
        return carry

    lax.fori_loop(0, TM, start, 0)
    lax.fori_loop(0, TM, wait, 0)
    o_ref[...] = x_ref[...] + mod_ref[0, 5:6] * (buf[0] + buf[1])


def _combine(lay, ys, pos, x, mod):
    d = x.shape[1]
    return pl.pallas_call(
        _combine_kernel,
        grid=(lay.n_blocks,),
        in_specs=[pl.BlockSpec((1, 2, TM), lambda j: (j, 0, 0), memory_space=pltpu.SMEM),
                  pl.BlockSpec(memory_space=pl.ANY), lay.row_spec(d), lay.mod_spec(d)],
        out_specs=lay.row_spec(d),
        out_shape=jax.ShapeDtypeStruct((lay.t_all, d), F32),
        scratch_shapes=[pltpu.VMEM((2, TM, d), F32), pltpu.SemaphoreType.DMA(())],
        compiler_params=_params(1),
        name="moe_combine",
    )(pos, ys, x, mod)


def _dispatch_plan(route):
    n_tok = route.shape[0]
    expert = route[:, 0:2].astype(jnp.int32).reshape(-1)
    weight = route[:, 2:4].reshape(-1)
    token = jnp.repeat(jnp.arange(n_tok, dtype=jnp.int32), 2)
    n_assign = n_tok * 2
    n_blocks = -(-n_assign // MOE_ROWS) + N_EXPERTS
    order = jnp.argsort(expert)
    e_sorted = expert[order]
    counts = jnp.zeros((N_EXPERTS,), jnp.int32).at[expert].add(1)
    start = jnp.cumsum(counts) - counts
    padded = (counts + MOE_ROWS - 1) // MOE_ROWS * MOE_ROWS
    p_end = jnp.cumsum(padded)
    dest = (p_end - padded)[e_sorted] + jnp.arange(n_assign, dtype=jnp.int32) - start[e_sorted]
    tok_pad = jnp.zeros((n_blocks * MOE_ROWS,), jnp.int32).at[dest].set(token[order])
    w_pad = jnp.zeros((n_blocks * MOE_ROWS,), F32).at[dest].set(weight[order])
    blk_expert = jnp.minimum(
        jnp.searchsorted(p_end, jnp.arange(n_blocks, dtype=jnp.int32) * MOE_ROWS, side='right'), N_EXPERTS - 1
    ).astype(jnp.int32)
    pos = jnp.zeros((n_assign,), jnp.int32).at[order].set(dest).reshape(n_tok, 2)
    return tok_pad, w_pad, blk_expert, pos


def _moe(lay, xn, h2, route, mod, w1, w3, w2):
    tok_pad, w_pad, blk_expert, pos = _dispatch_plan(route)
    ys = _expert_blocks(h2, tok_pad, w_pad, blk_expert, w1, w3, w2)
    pos_blocks = pos.reshape(lay.n_blocks, TM, 2).transpose(0, 2, 1)
    return _combine(lay, ys, pos_blocks, xn, mod)


def _rwkv_feat_kernel(x_ref, xp_ref, xq_ref, mod_ref, g_ref, mu_ref, wr_ref, wk_ref, wv_ref, w0_ref, w1_ref, w2_ref,
                      a0_ref, a1_ref, a2_ref, g1_ref, g2_ref, kk_ref, ka_ref, seg_ref,
                      r_ref, v_ref, kn_ref, gate_ref, lw_ref, kd_ref, b_ref, *, blocks_per_batch):
    jb = pl.program_id(0) % blocks_per_batch
    shift, scale, g = mod_ref[0, 0:1], mod_ref[0, 1:2], g_ref[...]
    h = _modulate(x_ref[...], g, shift, scale)
    has_prev = jnp.where(jb > 1, 1.0, 0.0)
    has_next = jnp.where(jnp.logical_and(jb > 0, jb < blocks_per_batch - 1), 1.0, 0.0)
    h_before = _modulate(xp_ref[...], g, shift, scale)[7:8] * has_prev
    h_after = _modulate(xq_ref[...], g, shift, scale)[0:1] * has_next
    rows = lax.broadcasted_iota(jnp.int32, h.shape, 0)
    prev = jnp.where(rows == 0, h_before, pltpu.roll(h, 1, 0))
    nxt = jnp.where(rows == TM - 1, h_after, pltpu.roll(h, TM - 1, 0))
    xx = 0.5 * (prev + nxt) - h
    mix = lambda i: (h + xx * mu_ref[i:i + 1]).astype(BF16)
    xr, xw, xk, xv, xa, xg = [mix(i) for i in range(6)]
    r = _dot(xr, wr_ref[...])
    k = _dot(xk, wk_ref[...])
    r_ref[...] = r
    v_ref[...] = _dot(xv, wv_ref[...])
    gate_ref[...] = _dot(_sigmoid(_dot(xg, g1_ref[...])).astype(BF16), g2_ref[...])
    kk = k * kk_ref[...]
    norm = jnp.sqrt(_seg_apply(kk * kk, seg_ref[...]))
    kk = kk / jnp.maximum(norm, 1e-12)
    kn_ref[...] = kk
    for n in range(2):
        w_log = w0_ref[n:n + 1] + _dot(jnp.tanh(_dot(xw, w1_ref[n])).astype(BF16), w2_ref[n])
        z = -w_log
        softplus = jnp.maximum(z, 0.0) + jnp.log(1.0 + jnp.exp(-jnp.abs(z)))
        lw_ref[n] = -jnp.exp(-softplus - 0.5)
        iclr = _sigmoid(a0_ref[n:n + 1] + _dot(_dot(xa, a1_ref[n]).astype(BF16), a2_ref[n]))
        kd_ref[n] = k * (1.0 + (iclr - 1.0) * ka_ref[...])
        b_ref[n] = kk * iclr


def _rwkv_features(lay, x, mod, g, p, seg_sum):
    (mu, w_r, w_k, w_v, w0, w1, w2, a0, a1, a2, g1, g2, k_k, k_a) = p
    d = x.shape[1]
    row = lay.row_spec(d)
    sub = TM // 8
    last = lay.t_all // 8 - 1
    before = pl.BlockSpec((8, d), lambda j: (jnp.maximum(j * sub - 1, 0), 0))
    after = pl.BlockSpec((8, d), lambda j: (jnp.minimum((j + 1) * sub, last), 0))
    both = pl.BlockSpec((2, TM, d), lambda j: (0, j, 0))
    bf = lambda t: t.astype(BF16)
    args = (x, x, x, mod, g.reshape(1, d), mu, bf(w_r), bf(w_k), bf(w_v), w0, bf(w1), bf(w2), a0, bf(a1), bf(a2),
            bf(g1), bf(g2), k_k.reshape(1, d), k_a.reshape(1, d), seg_sum)
    in_specs = [row, before, after, lay.mod_spec(d)] + [_full(a.shape) for a in args[4:]]
    shape = jax.ShapeDtypeStruct((lay.t_all, d), F32)
    shape2 = jax.ShapeDtypeStruct((2, lay.t_all, d), F32)
    return pl.pallas_call(
        functools.partial(_rwkv_feat_kernel, blocks_per_batch=lay.blocks_per_batch),
        grid=(lay.n_blocks,),
        in_specs=in_specs,
        out_specs=[row, row, row, row, both, both, both],
        out_shape=[shape, shape, shape, shape, shape2, shape2, shape2],
        compiler_params=_params(1),
        name="rwkv_features",
    )(*args)


def _scan_kernel(r_ref, v_ref, kk_ref, lw_ref, kd_ref, b_ref, y_ref, s_ref, *, n_heads):
    c = SCAN_CHUNK
    direction = pl.program_id(0)
    step = pl.program_id(2)

    @pl.when(step == 0)
    def _():
        s_ref[...] = jnp.zeros_like(s_ref)

    row = lax.broadcasted_iota(jnp.int32, (c, c), 0)
    col = lax.broadcasted_iota(jnp.int32, (c, c), 1)
    sign = 1 - 2 * direction
    lag = (row - col) * sign
    strict = lag > 0
    incl = lag >= 0

    lw = lw_ref[0]
    lg = _dot(incl.astype(F32), lw, HI)
    lg_tot = jnp.sum(lw, axis=0, keepdims=True)
    g_inv = jnp.exp(-lg)
    g_rem = jnp.exp(lg_tot - lg)
    a_t = -kk_ref[...] * jnp.exp(lg - lw)
    r_t = r_ref[...] * jnp.exp(lg)
    b_t = b_ref[0] * g_inv
    k_t = kd_ref[0] * g_inv
    b_h = b_ref[0] * g_rem
    k_h = kd_ref[0] * g_rem
    g_tot = jnp.exp(lg_tot)
    v_all = v_ref[...]

    for h in range(n_heads):
        sl = slice(h * HEAD, (h + 1) * HEAD)
        ar = jnp.concatenate([a_t[:, sl], r_t[:, sl]], axis=0)
        bk = jnp.concatenate([b_t[:, sl], k_t[:, sl]], axis=0)
        p = _dot_nt(ar, bk, HI)
        s0 = s_ref[h]
        ars = _dot_nt(ar, s0, HI)
        v_h = v_all[:, sl]
        lmat = jnp.where(strict, p[:c, :c], 0.0)
        x = ars[:c] + _dot(jnp.where(strict, p[:c, c:], 0.0), v_h, HI)
        x = x + _dot(lmat, x, HI)
        pw = lmat
        for _ in range(5):
            pw = _dot(pw, pw, HI)
            x = x + _dot(pw, x, HI)
        uv = jnp.concatenate([x, v_h], axis=0)
        m = jnp.concatenate([jnp.where(incl, p[c:, :c], 0.0), jnp.where(incl, p[c:, c:], 0.0)], axis=1)
        y_ref[0, :, sl] = ars[c:] + _dot(m, uv, HI)
        bkh = jnp.concatenate([b_h[:, sl], k_h[:, sl]], axis=0)
        s_ref[h] = s0 * g_tot[:, sl] + _dot_tn(uv, bkh, HI)


def _rwkv_scan(r, v, kk, lw, kd, b, *, n_batch, n_ctx):
    t_all, d = r.shape
    c = SCAN_CHUNK
    per_batch = t_all // n_batch // c
    ctx_chunks = n_ctx // c
    n_heads = d // HEAD

    def chunk(dr, bi, i):
        bwd_idx = jnp.where(i < ctx_chunks, ctx_chunks - 1 - i, per_batch - 1 - (i - ctx_chunks))
        return bi * per_batch + jnp.where(dr == 0, i, bwd_idx)

    shared = pl.BlockSpec((c, d), lambda dr, bi, i: (chunk(dr, bi, i), 0))
    per_dir = pl.BlockSpec((1, c, d), lambda dr, bi, i: (dr, chunk(dr, bi, i), 0))
    return pl.pallas_call(
        functools.partial(_scan_kernel, n_heads=n_heads),
        grid=(2, n_batch, per_batch),
        in_specs=[shared, shared, shared, per_dir, per_dir, per_dir],
        out_specs=per_dir,
        out_shape=jax.ShapeDtypeStruct((2, t_all, d), F32),
        scratch_shapes=[pltpu.VMEM((n_heads, HEAD, HEAD), F32)],
        compiler_params=_params(3),
        name="rwkv_scan",
    )(r, v, kk, lw, kd, b)


def _rwkv_out_kernel(y_ref, r_ref, v_ref, gate_ref, kd_ref, rk_ref, lnw_ref, lnb_ref, seg_ref, o_ref):
    seg = seg_ref[...]
    y = y_ref[0] + y_ref[1]
    mean = _seg_apply(y, seg)
    cen = y - mean
    var = _seg_apply(cen * cen, seg)
    yn = cen * lax.rsqrt(var + LN_X_EPS) * lnw_ref[...] + lnb_ref[...]
    bonus = _seg_apply(r_ref[...] * (kd_ref[0] + kd_ref[1]) * rk_ref[...], seg) * float(HEAD) * v_ref[...]
    o_ref[...] = ((yn + bonus) * gate_ref[...]).astype(BF16)


def _rwkv_out(lay, y2, r, v, gate, kd, r_k, ln_w, ln_b, seg_mean):
    d = r.shape[1]
    row = lay.row_spec(d)
    both = pl.BlockSpec((2, TM, d), lambda j: (0, j, 0))
    return pl.pallas_call(
        _rwkv_out_kernel,
        grid=(lay.n_blocks,),
        in_specs=[both, row, row, row, both, _full((1, d)), _full((1, d)), _full((1, d)), _full((LANES, LANES))],
        out_specs=row,
        out_shape=jax.ShapeDtypeStruct((lay.t_all, d), BF16),
        compiler_params=_params(1),
        name="rwkv_out",
    )(y2, r, v, gate, kd, r_k.reshape(1, d), ln_w.reshape(1, d), ln_b.reshape(1, d), seg_mean)


def _final_kernel(x_ref, g_ref, o_ref):
    x = x_ref[...]
    o_ref[0] = x * lax.rsqrt(jnp.mean(x * x, axis=-1, keepdims=True) + NORM_EPS) * g_ref[...]


def _final_norm(lay, x, g):
    d = x.shape[1]
    lat_blocks = lay.seq // TM
    return pl.pallas_call(
        _final_kernel,
        grid=(lay.n_batch, lat_blocks),
        in_specs=[pl.BlockSpec((TM, d), lambda b, j: (b * lay.blocks_per_batch + 1 + j, 0)), _full((1, d))],
        out_specs=pl.BlockSpec((1, TM, d), lambda b, j: (b, j, 0)),
        out_shape=jax.ShapeDtypeStruct((lay.n_batch, lay.seq, d), F32),
        compiler_params=_params(2),
        name="final_norm",
    )(x, g.reshape(1, d))


def kernel(x, c, ctx, c_ctx, mod_w, mod_b, norm1_g, norm2_g, a_w_qkv, a_w_o, a_q_gain, a_k_gain, b_mu, b_w_r, b_w_k, b_w_v, b_w_o, b_decay_w0, b_decay_w1, b_decay_w2, b_iclr_a0, b_iclr_a1, b_iclr_a2, b_gate_g1, b_gate_g2, b_k_k, b_k_a, b_r_k, b_ln_w, b_ln_b, c_w_qkv, c_w_o, c_sink, moe_w_group, moe_b_group, moe_w_expert, moe_b_expert, moe_w1, moe_w3, moe_w2, final_g):
    n_batch, seq, d = x.shape
    n_ctx = ctx.shape[1]
    depth = mod_w.shape[0]
    lay = _Layout(n_batch, n_ctx, seq)

    cond = jnp.zeros((8, d), F32).at[:n_batch].set(c).at[n_batch].set(c_ctx)
    mods = _mod_vectors(cond, mod_w, mod_b)[:, :n_batch + 1].reshape(depth, n_batch + 1, 6, d)
    xs = jnp.concatenate([ctx, x], axis=1).reshape(lay.t_all, d)

    blockdiag = (jnp.arange(LANES)[:, None] // HEAD == jnp.arange(LANES)[None, :] // HEAD).astype(F32)
    seg_sum, seg_mean = blockdiag, blockdiag / HEAD
    rope = _rope_tables(lay)
    no_gain = jnp.ones((HEAD,), F32)
    no_sink = jnp.zeros((d // HEAD,), F32)

    for l in range(depth):
        kind, j = l % 3, l // 3
        mod = mods[l]
        if kind == 0:
            q, k, v = _qkv_project(lay, xs, mod, norm1_g[l], a_w_qkv[j], rope, a_q_gain[j], a_k_gain[j], seg_mean,
                                   n_kv_heads=4, qk_norm=True)
            o = _attention(lay, q, k, v, no_sink, n_kv_heads=4, window=False)
            w_o = a_w_o[j]
        elif kind == 1:
            p = (b_mu[j], b_w_r[j], b_w_k[j], b_w_v[j], b_decay_w0[j], b_decay_w1[j], b_decay_w2[j],
                 b_iclr_a0[j], b_iclr_a1[j], b_iclr_a2[j], b_gate_g1[j], b_gate_g2[j], b_k_k[j], b_k_a[j])
            r, v, kk, gate, lw, kd, b = _rwkv_features(lay, xs, mod, norm1_g[l], p, seg_sum)
            y2 = _rwkv_scan(r, v, kk, lw, kd, b, n_batch=n_batch, n_ctx=n_ctx)
            o = _rwkv_out(lay, y2, r, v, gate, kd, b_r_k[j], b_ln_w[j], b_ln_b[j], seg_mean)
            w_o = b_w_o[j]
        else:
            q, k, v = _qkv_project(lay, xs, mod, norm1_g[l], c_w_qkv[j], rope, no_gain, no_gain, seg_mean,
                                   n_kv_heads=2, qk_norm=False)
            o = _attention(lay, q, k, v, c_sink[j], n_kv_heads=2, window=True)
            w_o = c_w_o[j]
        w_route = jnp.zeros((d, LANES), F32).at[:, :N_GROUPS].set(moe_w_group[l]) \
            .at[:, N_GROUPS:N_GROUPS + N_EXPERTS].set(moe_w_expert[l])
        b_route = jnp.zeros((1, LANES), F32).at[0, :N_GROUPS].set(moe_b_group[l]) \
            .at[0, N_GROUPS:N_GROUPS + N_EXPERTS].set(moe_b_expert[l])
        xn, h2, route = _post_mixer(lay, o, w_o, xs, mod, norm2_g[l], w_route, b_route)
        xs = _moe(lay, xn, h2, route, mod, moe_w1[l], moe_w3[l], moe_w2[l])
    return _final_norm(lay, xs, final_g)
```

```python
import functools
import math

import jax
import jax.numpy as jnp
from jax import lax
from jax.experimental import pallas as pl
from jax.experimental.pallas import tpu as pltpu

F32 = jnp.float32
BF16 = jnp.bfloat16
HI = lax.Precision.HIGHEST

HEAD = 64
LANES = 128
TM = 256
ATT_TQ = 128
ATT_TK = 2048
WINDOW = 128
SCAN_CHUNK = 64
MOE_ROWS = 256
N_GROUPS = 4
EXPERTS_PER_GROUP = 8
N_EXPERTS = N_GROUPS * EXPERTS_PER_GROUP
ROPE_THETA = 10000.0
NEG_INF = -1e30
Q_SCALE = HEAD ** -0.5
NORM_EPS = 1e-6
LN_X_EPS = 64e-5
VMEM_LIMIT = 56 * 1024 * 1024


def _params(n_axes):
    return pltpu.CompilerParams(dimension_semantics=("arbitrary",) * n_axes, vmem_limit_bytes=VMEM_LIMIT)


def _dot(a, b, precision=None):
    return jnp.dot(a, b, preferred_element_type=F32, precision=precision)


def _dot_nt(a, b, precision=None):
    return lax.dot_general(a, b, (((1,), (1,)), ((), ())), preferred_element_type=F32, precision=precision)


def _dot_tn(a, b, precision=None):
    return lax.dot_general(a, b, (((0,), (0,)), ((), ())), preferred_element_type=F32, precision=precision)


def _sigmoid(z):
    return 1.0 / (1.0 + jnp.exp(-z))


def _modulate(x, g, shift, scale):
    y = x * lax.rsqrt(jnp.mean(x * x, axis=-1, keepdims=True) + NORM_EPS)
    return (y * g) * (1.0 + scale) + shift


def _seg_apply(t, seg):
    n = t.shape[1] // LANES
    return jnp.concatenate([_dot(t[:, i * LANES:(i + 1) * LANES], seg, HI) for i in range(n)], axis=1)


def _mod_kernel(c_ref, w_ref, b_ref, o_ref):
    cond = c_ref[...]
    o_ref[0] = _dot(cond * _sigmoid(cond), w_ref[0], HI) + b_ref[0]


def _mod_vectors(cond, mod_w, mod_b):
    depth, d, n = mod_w.shape
    tn = 1536
    return pl.pallas_call(
        _mod_kernel,
        grid=(depth, n // tn),
        in_specs=[pl.BlockSpec(cond.shape, lambda l, j: (0, 0)),
                  pl.BlockSpec((1, d, tn), lambda l, j: (l, 0, j)),
                  pl.BlockSpec((1, 1, tn), lambda l, j: (l, 0, j))],
        out_specs=pl.BlockSpec((1, cond.shape[0], tn), lambda l, j: (l, 0, j)),
        out_shape=jax.ShapeDtypeStruct((depth, cond.shape[0], n), F32),
        compiler_params=_params(2),
        name="mod_vectors",
    )(cond, mod_w, mod_b.reshape(depth, 1, n))


class _Layout:
    def __init__(self, n_batch, n_ctx, seq):
        assert n_ctx == TM and seq % TM == 0
        self.n_batch, self.n_ctx, self.seq = n_batch, n_ctx, seq
        self.per_batch = n_ctx + seq
        self.blocks_per_batch = self.per_batch // TM
        self.n_blocks = n_batch * self.blocks_per_batch
        self.t_all = n_batch * self.per_batch

    def mod_index(self, j):
        jb = j % self.blocks_per_batch
        return jnp.where(jb == 0, self.n_batch, j // self.blocks_per_batch)

    def mod_spec(self, d):
        return pl.BlockSpec((1, 6, d), lambda j: (self.mod_index(j), 0, 0))

    def row_spec(self, width):
        return pl.BlockSpec((TM, width), lambda j: (j, 0))

    def pos_spec(self, width):
        return pl.BlockSpec((TM, width), lambda j: (j % self.blocks_per_batch, 0))


def _full(shape):
    return pl.BlockSpec(shape, lambda *_: (0,) * len(shape))


def _qkv_kernel(x_ref, mod_ref, g_ref, w_ref, cos_ref, sina_ref, sinb_ref, qg_ref, kg_ref, seg_ref,
                q_ref, k_ref, v_ref, *, n_q, n_kv, qk_norm):
    h = _modulate(x_ref[...], g_ref[...], mod_ref[0, 0:1], mod_ref[0, 1:2])
    y = _dot(h.astype(BF16), w_ref[...])
    cos, sina, sinb = cos_ref[...], sina_ref[...], sinb_ref[...]
    seg = seg_ref[...]
    lane = lax.broadcasted_iota(jnp.int32, (TM, LANES), 1)
    low = lane < HEAD
    ones_col = jnp.where(lane == HEAD, 1.0, 0.0)

    def head_pair(t, gain):
        if qk_norm:
            t = t * lax.rsqrt(_dot(t * t, seg, HI) + NORM_EPS) * gain
        return t * cos + pltpu.roll(t, LANES - 16, 1) * sina + pltpu.roll(t, 16, 1) * sinb

    def both_heads(t):
        return t, pltpu.roll(t, HEAD, 1)

    for i in range(n_q // LANES):
        t = head_pair(y[:, i * LANES:(i + 1) * LANES], qg_ref[...]) * Q_SCALE
        for u, th in enumerate(both_heads(t)):
            q_ref[:, (2 * i + u) * LANES:(2 * i + u + 1) * LANES] = th.astype(BF16)
    for i in range(n_kv // LANES):
        t = head_pair(y[:, n_q + i * LANES:n_q + (i + 1) * LANES], kg_ref[...])
        for u, th in enumerate(both_heads(t)):
            k_ref[:, (2 * i + u) * LANES:(2 * i + u + 1) * LANES] = jnp.where(low, th, 0.0).astype(BF16)
        t = y[:, n_q + n_kv + i * LANES:n_q + n_kv + (i + 1) * LANES]
        for u, th in enumerate(both_heads(t)):
            v_ref[:, (2 * i + u) * LANES:(2 * i + u + 1) * LANES] = jnp.where(low, th, ones_col).astype(BF16)


def _qkv_project(lay, x, mod, g, w, rope, q_gain, k_gain, seg_mean, *, n_kv_heads, qk_norm):
    d = x.shape[1]
    n_q, n_kv = d, n_kv_heads * HEAD
    cos, sina, sinb = rope
    pair = lambda v: jnp.tile(v.reshape(1, HEAD), (1, LANES // HEAD))
    return pl.pallas_call(
        functools.partial(_qkv_kernel, n_q=n_q, n_kv=n_kv, qk_norm=qk_norm),
        grid=(lay.n_blocks,),
        in_specs=[lay.row_spec(d), lay.mod_spec(d), _full((1, d)), _full(w.shape),
                  lay.pos_spec(LANES), lay.pos_spec(LANES), lay.pos_spec(LANES),
                  _full((1, LANES)), _full((1, LANES)), _full((LANES, LANES))],
        out_specs=[lay.row_spec(2 * n_q), lay.row_spec(2 * n_kv), lay.row_spec(2 * n_kv)],
        out_shape=[jax.ShapeDtypeStruct((lay.t_all, 2 * n_q), BF16),
                   jax.ShapeDtypeStruct((lay.t_all, 2 * n_kv), BF16),
                   jax.ShapeDtypeStruct((lay.t_all, 2 * n_kv), BF16)],
        compiler_params=_params(1),
        name="qkv_project",
    )(x, mod, g.reshape(1, d), w.astype(BF16), cos, sina, sinb, pair(q_gain), pair(k_gain), seg_mean)


def _rope_tables(lay):
    rows = lay.seq // 64
    grid_w = 64
    row = jnp.repeat(jnp.arange(rows, dtype=F32), grid_w)
    col = jnp.tile(jnp.arange(grid_w, dtype=F32), rows)
    axis_dim = HEAD // 2
    inv_freq = ROPE_THETA ** (-jnp.arange(0, axis_dim, 2, dtype=F32) / axis_dim)
    a_row = row[:, None] * inv_freq
    a_col = col[:, None] * inv_freq
    ang = jnp.concatenate([a_row, a_row, a_col, a_col], axis=-1)
    cos = jnp.concatenate([jnp.ones((lay.n_ctx, HEAD), F32), jnp.cos(ang)], axis=0)
    sin = jnp.concatenate([jnp.zeros((lay.n_ctx, HEAD), F32), jnp.sin(ang)], axis=0)
    first = (jnp.arange(HEAD) % 32) < 16
    sina = jnp.where(first, -sin, 0.0)
    sinb = jnp.where(first, 0.0, sin)
    wide = lambda t: jnp.tile(t, (1, LANES // HEAD))
    return wide(cos), wide(sina), wide(sinb)


def _attn_kernel(sink_ref, q_ref, k_ref, v_ref, o_ref, *, n_kv, grp, window, n_ctx, seq):
    tq = ATT_TQ
    j = pl.program_id(1)
    rows = grp * tq
    ctx_blocks = n_ctx // tq
    gs = range(n_kv)
    tile = lambda ref, r0, g: ref[r0, g * LANES:(g + 1) * LANES]
    qg = [jnp.concatenate([tile(q_ref, slice(None), g * grp + u) for u in range(grp)], axis=0) for g in gs]

    def step(carry, ks, vs, keep=None):
        s = [_dot_nt(qg[g], ks[g]) for g in gs]
        if keep is not None:
            s = [jnp.where(keep, s[g], NEG_INF) for g in gs]
        m_new = [jnp.maximum(carry[g][0], jnp.max(s[g], axis=1, keepdims=True)) for g in gs]
        p = [jnp.exp(s[g] - m_new[g]).astype(BF16) for g in gs]
        alpha = [jnp.exp(carry[g][0] - m_new[g]) for g in gs]
        pv = [_dot(p[g], vs[g]) for g in gs]
        return tuple((m_new[g], alpha[g] * carry[g][1] + pv[g]) for g in gs)

    if window:
        lane = lax.broadcasted_iota(jnp.int32, (rows, LANES), 1)
        acc0 = jnp.where(lane == HEAD, 1.0, 0.0)
        carry = tuple((jnp.concatenate([jnp.full((tq, 1), sink_ref[g * grp + u], F32) for u in range(grp)],
                                       axis=0), acc0) for g in gs)
        tk = WINDOW
        rr = lax.broadcasted_iota(jnp.int32, (rows, tk), 0) & (tq - 1)
        cc = lax.broadcasted_iota(jnp.int32, (rows, tk), 1)
        j_lat = j - ctx_blocks
        n_lat = seq // tq
        never = 4 * tk
        for cb in range(n_ctx // tk):
            r0 = slice(cb * tk, (cb + 1) * tk)
            carry = step(carry, [tile(k_ref, r0, g) for g in gs], [tile(v_ref, r0, g) for g in gs])
        for off in (-1, 0, 1):
            blk = j_lat + off
            valid = jnp.logical_and(j_lat >= 0, jnp.logical_and(blk >= 0, blk < n_lat))
            r0 = pl.ds(pl.multiple_of(n_ctx + jnp.clip(blk, 0, n_lat - 1) * tk, tk), tk)
            bar = jnp.where(valid, 0, never)
            keep = {-1: cc >= rr + bar, 0: cc >= bar, 1: cc + bar <= rr}[off]
            carry = step(carry, [tile(k_ref, r0, g) for g in gs], [tile(v_ref, r0, g) for g in gs], keep)
    else:
        tk = min(ATT_TK, seq)
        assert seq % tk == 0
        carry = tuple((jnp.full((rows, 1), NEG_INF, F32), jnp.zeros((rows, LANES), F32)) for g in gs)
        ctx_rows = slice(0, n_ctx)
        carry = step(carry, [tile(k_ref, ctx_rows, g) for g in gs], [tile(v_ref, ctx_rows, g) for g in gs])
        n_tiles = jnp.where(j < ctx_blocks, 0, seq // tk)

        def body(t, c):
            r0 = pl.ds(pl.multiple_of(n_ctx + t * tk, math.gcd(n_ctx, tk)), tk)
            return step(c, [tile(k_ref, r0, g) for g in gs], [tile(v_ref, r0, g) for g in gs])

        carry = lax.fori_loop(0, n_tiles, body, carry)
    for g in gs:
        acc = carry[g][1]
        out = acc[:, :HEAD] / acc[:, HEAD:HEAD + 1]
        for u in range(grp):
            hd = g * grp + u
            o_ref[:, hd * HEAD:(hd + 1) * HEAD] = out[u * tq:(u + 1) * tq].astype(BF16)


def _attention(lay, q, k, v, sink, *, n_kv_heads, window):
    d = q.shape[1] // 2
    n_kv = n_kv_heads * LANES
    per_q = lay.per_batch // ATT_TQ
    kv_spec = pl.BlockSpec((lay.per_batch, n_kv), lambda b, j, *_: (b, 0), pipeline_mode=pl.Buffered(1))
    q_spec = pl.BlockSpec((ATT_TQ, 2 * d), lambda b, j, *_: (b * per_q + j, 0))
    o_spec = pl.BlockSpec((ATT_TQ, d), lambda b, j, *_: (b * per_q + j, 0))
    return pl.pallas_call(
        functools.partial(_attn_kernel, n_kv=n_kv_heads, grp=d // HEAD // n_kv_heads, window=window,
                          n_ctx=lay.n_ctx, seq=lay.seq),
        grid_spec=pltpu.PrefetchScalarGridSpec(
            num_scalar_prefetch=1, grid=(lay.n_batch, per_q),
            in_specs=[q_spec, kv_spec, kv_spec], out_specs=o_spec),
        out_shape=jax.ShapeDtypeStruct((lay.t_all, d), BF16),
        compiler_params=_params(2),
        name="window_attention" if window else "dense_attention",
    )(sink, q, k, v)


def _route(logits):
    lane = lax.broadcasted_iota(jnp.int32, logits.shape, 1)
    far = 4 * LANES
    is_g = lane < N_GROUPS
    gmax = jnp.max(jnp.where(is_g, logits, NEG_INF), axis=1, keepdims=True)
    gidx = jnp.min(jnp.where(jnp.logical_and(is_g, logits == gmax), lane, far), axis=1, keepdims=True)
    gsum = jnp.sum(jnp.where(is_g, jnp.exp(logits - gmax), 0.0), axis=1, keepdims=True)
    g_w = 1.0 / gsum
    is_e = jnp.logical_and(lane >= N_GROUPS, ((lane - N_GROUPS) >> 3) == gidx)
    emax = jnp.max(jnp.where(is_e, logits, NEG_INF), axis=1, keepdims=True)
    ee = jnp.where(is_e, jnp.exp(logits - emax), 0.0)
    p = ee / jnp.sum(ee, axis=1, keepdims=True)
    p_in = jnp.where(is_e, p, -1.0)
    p1 = jnp.max(p_in, axis=1, keepdims=True)
    i1 = jnp.min(jnp.where(p_in == p1, lane, far), axis=1, keepdims=True)
    p_rest = jnp.where(lane == i1, -1.0, p_in)
    p2 = jnp.max(p_rest, axis=1, keepdims=True)
    i2 = jnp.min(jnp.where(p_rest == p2, lane, far), axis=1, keepdims=True)
    tot = p1 + p2
    w1 = g_w * p1 / tot
    w2 = g_w * p2 / tot
    e1 = (i1 - N_GROUPS).astype(F32)
    e2 = (i2 - N_GROUPS).astype(F32)
    picks = (jnp.where(lane == i1, 1.0, 0.0), jnp.where(lane == i2, 1.0, 0.0))
    return (e1, e2, w1, w2), picks


def _post_kernel(o_ref, wo_ref, x_ref, mod_ref, g2_ref, wr_ref, br_ref, xn_ref, h2_ref, route_ref, count_ref, seen_ref):
    @pl.when(pl.program_id(0) == 0)
    def _():
        seen_ref[...] = jnp.zeros_like(seen_ref)

    y = _dot(o_ref[...], wo_ref[...])
    xn = x_ref[...] + mod_ref[0, 2:3] * y
    xn_ref[...] = xn
    h2 = _modulate(xn, g2_ref[...], mod_ref[0, 3:4], mod_ref[0, 4:5])
    h2_ref[...] = h2
    cols, (pick1, pick2) = _route(_dot(h2, wr_ref[...], HI) + br_ref[...])
    both = pick1 + pick2
    row = lax.broadcasted_iota(jnp.int32, (TM, TM), 0)
    col = lax.broadcasted_iota(jnp.int32, (TM, TM), 1)
    earlier = jnp.where(col < row, 1.0, 0.0).astype(BF16)
    before = _dot(earlier, both.astype(BF16)) + seen_ref[0:1]
    ranks = (jnp.sum(before * pick1, axis=1, keepdims=True), jnp.sum(before * pick2, axis=1, keepdims=True))
    seen = seen_ref[0:1] + jnp.sum(both, axis=0, keepdims=True)
    seen_ref[...] = jnp.broadcast_to(seen, seen_ref.shape)
    count_ref[...] = jnp.broadcast_to(seen, count_ref.shape)
    lane = lax.broadcasted_iota(jnp.int32, (TM, LANES), 1)
    out = jnp.zeros((TM, LANES), F32)
    for i, c in enumerate(cols + ranks):
        out = jnp.where(lane == i, c, out)
    route_ref[...] = out


def _post_mixer(lay, o, w_o, x, mod, g2, w_route, b_route):
    d = x.shape[1]
    return pl.pallas_call(
        _post_kernel,
        grid=(lay.n_blocks,),
        in_specs=[lay.row_spec(d), _full((d, d)), lay.row_spec(d), lay.mod_spec(d), _full((1, d)),
                  _full((d, LANES)), _full((1, LANES))],
        out_specs=[lay.row_spec(d), lay.row_spec(d), lay.row_spec(LANES), _full((8, LANES))],
        out_shape=[jax.ShapeDtypeStruct((lay.t_all, d), F32), jax.ShapeDtypeStruct((lay.t_all, d), F32),
                   jax.ShapeDtypeStruct((lay.t_all, LANES), F32), jax.ShapeDtypeStruct((8, LANES), F32)],
        scratch_shapes=[pltpu.VMEM((8, LANES), F32)],
        compiler_params=_params(1),
        name="post_mixer",
    )(o, w_o.astype(BF16), x, mod, g2.reshape(1, d), w_route, b_route)


def _dispatch_kernel(dest_ref, h_ref, xs_in, xs_hbm, sem):
    del xs_in

    def row_copy(r, k):
        return pltpu.make_async_copy(h_ref.at[pl.ds(r, 1)], xs_hbm.at[pl.ds(dest_ref[0, k, r], 1)], sem)

    def start(r, carry):
        row_copy(r, 0).start()
        row_copy(r, 1).start()
        return carry

    def wait(r, carry):
        row_copy(r, 0).wait()
        row_copy(r, 1).wait()
        return carry

    lax.fori_loop(0, TM, start, 0)
    lax.fori_loop(0, TM, wait, 0)


def _dispatch(lay, h2, dest_blocks, n_rows):
    d = h2.shape[1]
    return pl.pallas_call(
        _dispatch_kernel,
        grid=(lay.n_blocks,),
        in_specs=[pl.BlockSpec((1, 2, TM), lambda j: (j, 0, 0), memory_space=pltpu.SMEM), lay.row_spec(d),
                  pl.BlockSpec(memory_space=pl.ANY)],
        out_specs=pl.BlockSpec(memory_space=pl.ANY),
        out_shape=jax.ShapeDtypeStruct((n_rows, d), F32),
        scratch_shapes=[pltpu.SemaphoreType.DMA(())],
        input_output_aliases={2: 0},
        compiler_params=_params(1),
        name="moe_dispatch",
    )(dest_blocks, h2, jnp.zeros((n_rows, d), F32))


def _expert_kernel(blk_ref, used_ref, x_ref, w1_ref, w3_ref, w2_ref, out_ref):
    del blk_ref
    i = pl.program_id(0)

    @pl.when(i < used_ref[0])
    def _():
        xb = x_ref[...].astype(BF16)
        a = _dot(xb, w1_ref[0].astype(BF16))
        hid = (a * _sigmoid(a)) * _dot(xb, w3_ref[0].astype(BF16))
        out_ref[...] = _dot(hid.astype(BF16), w2_ref[0].astype(BF16))

    @pl.when(i >= used_ref[0])
    def _():
        out_ref[...] = jnp.zeros_like(out_ref)


def _expert_blocks(xs, blk_expert, n_used, w1, w3, w2, layer):
    n_rows, d = xs.shape
    n_blocks = n_rows // MOE_ROWS
    de = w1.shape[2]
    base = layer * N_EXPERTS
    return pl.pallas_call(
        _expert_kernel,
        grid_spec=pltpu.PrefetchScalarGridSpec(
            num_scalar_prefetch=2, grid=(n_blocks,),
            in_specs=[pl.BlockSpec((MOE_ROWS, d), lambda i, blk, used: (i, 0)),
                      pl.BlockSpec((1, d, de), lambda i, blk, used: (base + blk[i], 0, 0)),
                      pl.BlockSpec((1, d, de), lambda i, blk, used: (base + blk[i], 0, 0)),
                      pl.BlockSpec((1, de, d), lambda i, blk, used: (base + blk[i], 0, 0))],
            out_specs=pl.BlockSpec((MOE_ROWS, d), lambda i, blk, used: (i, 0))),
        out_shape=jax.ShapeDtypeStruct((n_rows, d), F32),
        compiler_params=_params(1),
        name="expert_blocks",
    )(blk_expert, n_used, xs, w1, w3, w2)


def _combine_kernel(pos_ref, ys_hbm, x_ref, mod_ref, route_ref, o_ref, buf, sem):
    def row_copy(r, k):
        return pltpu.make_async_copy(ys_hbm.at[pl.ds(pos_ref[0, k, r], 1)], buf.at[k, pl.ds(r, 1)], sem)

    def start(r, carry):
        row_copy(r, 0).start()
        row_copy(r, 1).start()
        return carry

    def wait(r, carry):
        row_copy(r, 0).wait()
        row_copy(r, 1).wait()
        return carry

    lax.fori_loop(0, TM, start, 0)
    lax.fori_loop(0, TM, wait, 0)
    route = route_ref[...]
    f = buf[0] * route[:, 2:3] + buf[1] * route[:, 3:4]
    o_ref[...] = x_ref[...] + mod_ref[0, 5:6] * f


def _combine(lay, ys, pos, x, mod, route):
    d = x.shape[1]
    return pl.pallas_call(
        _combine_kernel,
        grid=(lay.n_blocks,),
        in_specs=[pl.BlockSpec((1, 2, TM), lambda j: (j, 0, 0), memory_space=pltpu.SMEM),
                  pl.BlockSpec(memory_space=pl.ANY), lay.row_spec(d), lay.mod_spec(d), lay.row_spec(LANES)],
        out_specs=lay.row_spec(d),
        out_shape=jax.ShapeDtypeStruct((lay.t_all, d), F32),
        scratch_shapes=[pltpu.VMEM((2, TM, d), F32), pltpu.SemaphoreType.DMA(())],
        compiler_params=_params(1),
        name="moe_combine",
    )(pos, ys, x, mod, route)


def _dispatch_plan(route, counts):
    n_tok = route.shape[0]
    n_blocks = -(-(n_tok * 2) // MOE_ROWS) + N_EXPERTS
    counts = counts[0, N_GROUPS:N_GROUPS + N_EXPERTS].astype(jnp.int32)
    padded = (counts + MOE_ROWS - 1) // MOE_ROWS * MOE_ROWS
    p_end = jnp.cumsum(padded)
    first_slot = p_end - padded
    expert = route[:, 0:2].astype(jnp.int32)
    onehot = expert[:, :, None] == jnp.arange(N_EXPERTS, dtype=jnp.int32)
    dest = jnp.sum(jnp.where(onehot, first_slot, 0), axis=-1) + route[:, 4:6].astype(jnp.int32)
    block_start = jnp.arange(n_blocks, dtype=jnp.int32) * MOE_ROWS
    blk_expert = jnp.minimum(jnp.sum(p_end[None, :] <= block_start[:, None], axis=1), N_EXPERTS - 1).astype(jnp.int32)
    n_used = (p_end[-1:] // MOE_ROWS).astype(jnp.int32)
    return dest, blk_expert, n_used, n_blocks


def _moe(lay, xn, h2, route, counts, mod, w1, w3, w2, layer):
    dest, blk_expert, n_used, n_blocks = _dispatch_plan(route, counts)
    dest_blocks = dest.reshape(lay.n_blocks, TM, 2).transpose(0, 2, 1)
    xs = _dispatch(lay, h2, dest_blocks, n_blocks * MOE_ROWS)
    ys = _expert_blocks(xs, blk_expert, n_used, w1, w3, w2, layer)
    return _combine(lay, ys, dest_blocks, xn, mod, route)


def _rwkv_feat_kernel(x_ref, xp_ref, xq_ref, mod_ref, g_ref, mu_ref, wr_ref, wk_ref, wv_ref, w0_ref, w1_ref, w2_ref,
                      a0_ref, a1_ref, a2_ref, g1_ref, g2_ref, kk_ref, ka_ref, seg_ref,
                      r_ref, v_ref, kn_ref, gate_ref, lw_ref, kd_ref, b_ref, *, blocks_per_batch):
    jb = pl.program_id(0) % blocks_per_batch
    shift, scale, g = mod_ref[0, 0:1], mod_ref[0, 1:2], g_ref[...]
    h = _modulate(x_ref[...], g, shift, scale)
    has_prev = jnp.where(jb > 1, 1.0, 0.0)
    has_next = jnp.where(jnp.logical_and(jb > 0, jb < blocks_per_batch - 1), 1.0, 0.0)
    h_before = _modulate(xp_ref[...], g, shift, scale)[7:8] * has_prev
    h_after = _modulate(xq_ref[...], g, shift, scale)[0:1] * has_next
    rows = lax.broadcasted_iota(jnp.int32, h.shape, 0)
    prev = jnp.where(rows == 0, h_before, pltpu.roll(h, 1, 0))
    nxt = jnp.where(rows == TM - 1, h_after, pltpu.roll(h, TM - 1, 0))
    xx = 0.5 * (prev + nxt) - h
    mix = lambda i: (h + xx * mu_ref[i:i + 1]).astype(BF16)
    xr, xw, xk, xv, xa, xg = [mix(i) for i in range(6)]
    r = _dot(xr, wr_ref[...])
    k = _dot(xk, wk_ref[...])
    r_ref[...] = r
    v_ref[...] = _dot(xv, wv_ref[...])
    gate_ref[...] = _dot(_sigmoid(_dot(xg, g1_ref[...])).astype(BF16), g2_ref[...])
    kk = k * kk_ref[...]
    norm = jnp.sqrt(_seg_apply(kk * kk, seg_ref[...]))
    kk = kk / jnp.maximum(norm, 1e-12)
    kn_ref[...] = kk
    for n in range(2):
        w_log = w0_ref[n:n + 1] + _dot(jnp.tanh(_dot(xw, w1_ref[n])).astype(BF16), w2_ref[n])
        z = -w_log
        softplus = jnp.maximum(z, 0.0) + jnp.log(1.0 + jnp.exp(-jnp.abs(z)))
        lw_ref[n] = -jnp.exp(-softplus - 0.5)
        iclr = _sigmoid(a0_ref[n:n + 1] + _dot(_dot(xa, a1_ref[n]).astype(BF16), a2_ref[n]))
        kd_ref[n] = k * (1.0 + (iclr - 1.0) * ka_ref[...])
        b_ref[n] = kk * iclr


def _rwkv_features(lay, x, mod, g, p, seg_sum):
    (mu, w_r, w_k, w_v, w0, w1, w2, a0, a1, a2, g1, g2, k_k, k_a) = p
    d = x.shape[1]
    row = lay.row_spec(d)
    sub = TM // 8
    last = lay.t_all // 8 - 1
    before = pl.BlockSpec((8, d), lambda j: (jnp.maximum(j * sub - 1, 0), 0))
    after = pl.BlockSpec((8, d), lambda j: (jnp.minimum((j + 1) * sub, last), 0))
    both = pl.BlockSpec((2, TM, d), lambda j: (0, j, 0))
    bf = lambda t: t.astype(BF16)
    args = (x, x, x, mod, g.reshape(1, d), mu, bf(w_r), bf(w_k), bf(w_v), w0, bf(w1), bf(w2), a0, bf(a1), bf(a2),
            bf(g1), bf(g2), k_k.reshape(1, d), k_a.reshape(1, d), seg_sum)
    in_specs = [row, before, after, lay.mod_spec(d)] + [_full(a.shape) for a in args[4:]]
    shape = jax.ShapeDtypeStruct((lay.t_all, d), F32)
    shape2 = jax.ShapeDtypeStruct((2, lay.t_all, d), F32)
    return pl.pallas_call(
        functools.partial(_rwkv_feat_kernel, blocks_per_batch=lay.blocks_per_batch),
        grid=(lay.n_blocks,),
        in_specs=in_specs,
        out_specs=[row, row, row, row, both, both, both],
        out_shape=[shape, shape, shape, shape, shape2, shape2, shape2],
        compiler_params=_params(1),
        name="rwkv_features",
    )(*args)


def _scan_kernel(r_ref, v_ref, kk_ref, lw_ref, kd_ref, b_ref, y_ref, s_ref, *, n_heads):
    c = SCAN_CHUNK
    direction = pl.program_id(0)
    step = pl.program_id(2)

    @pl.when(step == 0)
    def _():
        s_ref[...] = jnp.zeros_like(s_ref)

    row = lax.broadcasted_iota(jnp.int32, (c, c), 0)
    col = lax.broadcasted_iota(jnp.int32, (c, c), 1)
    sign = 1 - 2 * direction
    lag = (row - col) * sign
    strict = lag > 0
    incl = lag >= 0

    lw = lw_ref[0]
    lg = _dot(incl.astype(F32), lw, HI)
    lg_tot = jnp.sum(lw, axis=0, keepdims=True)
    g_inv = jnp.exp(-lg)
    g_rem = jnp.exp(lg_tot - lg)
    a_t = -kk_ref[...] * jnp.exp(lg - lw)
    r_t = r_ref[...] * jnp.exp(lg)
    b_t = b_ref[0] * g_inv
    k_t = kd_ref[0] * g_inv
    b_h = b_ref[0] * g_rem
    k_h = kd_ref[0] * g_rem
    g_tot = jnp.exp(lg_tot)
    v_all = v_ref[...]

    hs = range(n_heads)
    sl = [slice(h * HEAD, (h + 1) * HEAD) for h in hs]
    cat = lambda x, y: [jnp.concatenate([x[:, s_], y[:, s_]], axis=0) for s_ in sl]
    ar, bk, bkh = cat(a_t, r_t), cat(b_t, k_t), cat(b_h, k_h)
    v_h = [v_all[:, s_] for s_ in sl]
    s0 = [s_ref[h] for h in hs]
    p = [_mm("P", ar[h], bk[h], "nt") for h in hs]
    ars = [_mm("S", ar[h], s0[h], "nt") for h in hs]
    x = [ars[h][:c] + _mm("V", jnp.where(strict, p[h][:c, c:], 0.0), v_h[h]) for h in hs]
    pw = [jnp.where(strict, p[h][:c, :c], 0.0) for h in hs]
    x = [x[h] + _mm("L", pw[h], x[h]) for h in hs]
    for _ in range(5):
        pw = [_mm("L", pw[h], pw[h]) for h in hs]
        x = [x[h] + _mm("L", pw[h], x[h]) for h in hs]
    uv = [jnp.concatenate([x[h], v_h[h]], axis=0) for h in hs]
    m = [jnp.concatenate([jnp.where(incl, p[h][c:, :c], 0.0), jnp.where(incl, p[h][c:, c:], 0.0)], axis=1) for h in hs]
    y_ref[0] = jnp.concatenate([ars[h][c:] + _mm("Y", m[h], uv[h]) for h in hs], axis=1)
    states = [s0[h] * g_tot[:, sl[h]] + _mm("Z", uv[h], bkh[h], "tn") for h in hs]
    for h in hs:
        s_ref[h] = states[h]


def _mm(kind, a, b, form="nn"):
    del kind
    dn = {"nn": (((1,), (0,)), ((), ())), "nt": (((1,), (1,)), ((), ())), "tn": (((0,), (0,)), ((), ()))}[form]
    return lax.dot_general(a.astype(BF16), b.astype(BF16), dn, preferred_element_type=F32)


def _rwkv_scan(r, v, kk, lw, kd, b, *, n_batch, n_ctx):
    t_all, d = r.shape
    c = SCAN_CHUNK
    per_batch = t_all // n_batch // c
    ctx_chunks = n_ctx // c
    n_heads = d // HEAD

    def chunk(dr, bi, i):
        bwd_idx = jnp.where(i < ctx_chunks, ctx_chunks - 1 - i, per_batch - 1 - (i - ctx_chunks))
        return bi * per_batch + jnp.where(dr == 0, i, bwd_idx)

    shared = pl.BlockSpec((c, d), lambda dr, bi, i: (chunk(dr, bi, i), 0))
    per_dir = pl.BlockSpec((1, c, d), lambda dr, bi, i: (dr, chunk(dr, bi, i), 0))
    return pl.pallas_call(
        functools.partial(_scan_kernel, n_heads=n_heads),
        grid=(2, n_batch, per_batch),
        in_specs=[shared, shared, shared, per_dir, per_dir, per_dir],
        out_specs=per_dir,
        out_shape=jax.ShapeDtypeStruct((2, t_all, d), F32),
        scratch_shapes=[pltpu.VMEM((n_heads, HEAD, HEAD), F32)],
        compiler_params=_params(3),
        name="rwkv_scan",
    )(r, v, kk, lw, kd, b)


def _rwkv_out_kernel(y_ref, r_ref, v_ref, gate_ref, kd_ref, rk_ref, lnw_ref, lnb_ref, seg_ref, o_ref):
    seg = seg_ref[...]
    y = y_ref[0] + y_ref[1]
    mean = _seg_apply(y, seg)
    cen = y - mean
    var = _seg_apply(cen * cen, seg)
    yn = cen * lax.rsqrt(var + LN_X_EPS) * lnw_ref[...] + lnb_ref[...]
    bonus = _seg_apply(r_ref[...] * (kd_ref[0] + kd_ref[1]) * rk_ref[...], seg) * float(HEAD) * v_ref[...]
    o_ref[...] = ((yn + bonus) * gate_ref[...]).astype(BF16)


def _rwkv_out(lay, y2, r, v, gate, kd, r_k, ln_w, ln_b, seg_mean):
    d = r.shape[1]
    row = lay.row_spec(d)
    both = pl.BlockSpec((2, TM, d), lambda j: (0, j, 0))
    return pl.pallas_call(
        _rwkv_out_kernel,
        grid=(lay.n_blocks,),
        in_specs=[both, row, row, row, both, _full((1, d)), _full((1, d)), _full((1, d)), _full((LANES, LANES))],
        out_specs=row,
        out_shape=jax.ShapeDtypeStruct((lay.t_all, d), BF16),
        compiler_params=_params(1),
        name="rwkv_out",
    )(y2, r, v, gate, kd, r_k.reshape(1, d), ln_w.reshape(1, d), ln_b.reshape(1, d), seg_mean)


def _final_kernel(x_ref, g_ref, o_ref):
    x = x_ref[...]
    o_ref[0] = x * lax.rsqrt(jnp.mean(x * x, axis=-1, keepdims=True) + NORM_EPS) * g_ref[...]


def _final_norm(lay, x, g):
    d = x.shape[1]
    lat_blocks = lay.seq // TM
    return pl.pallas_call(
        _final_kernel,
        grid=(lay.n_batch, lat_blocks),
        in_specs=[pl.BlockSpec((TM, d), lambda b, j: (b * lay.blocks_per_batch + 1 + j, 0)), _full((1, d))],
        out_specs=pl.BlockSpec((1, TM, d), lambda b, j: (b, j, 0)),
        out_shape=jax.ShapeDtypeStruct((lay.n_batch, lay.seq, d), F32),
        compiler_params=_params(2),
        name="final_norm",
    )(x, g.reshape(1, d))


def kernel(x, c, ctx, c_ctx, mod_w, mod_b, norm1_g, norm2_g, a_w_qkv, a_w_o, a_q_gain, a_k_gain, b_mu, b_w_r, b_w_k, b_w_v, b_w_o, b_decay_w0, b_decay_w1, b_decay_w2, b_iclr_a0, b_iclr_a1, b_iclr_a2, b_gate_g1, b_gate_g2, b_k_k, b_k_a, b_r_k, b_ln_w, b_ln_b, c_w_qkv, c_w_o, c_sink, moe_w_group, moe_b_group, moe_w_expert, moe_b_expert, moe_w1, moe_w3, moe_w2, final_g):
    n_batch, seq, d = x.shape
    n_ctx = ctx.shape[1]
    depth = mod_w.shape[0]
    lay = _Layout(n_batch, n_ctx, seq)

    cond = jnp.zeros((8, d), F32).at[:n_batch].set(c).at[n_batch].set(c_ctx)
    mods = _mod_vectors(cond, mod_w, mod_b)[:, :n_batch + 1].reshape(depth, n_batch + 1, 6, d)
    xs = jnp.concatenate([ctx, x], axis=1).reshape(lay.t_all, d)

    blockdiag = (jnp.arange(LANES)[:, None] // HEAD == jnp.arange(LANES)[None, :] // HEAD).astype(F32)
    seg_sum, seg_mean = blockdiag, blockdiag / HEAD
    rope = _rope_tables(lay)
    no_gain = jnp.ones((HEAD,), F32)
    stacked = lambda w: w.reshape((-1,) + w.shape[2:])
    w1_all, w3_all, w2_all = stacked(moe_w1), stacked(moe_w3), stacked(moe_w2)
    no_sink = jnp.zeros((d // HEAD,), F32)

    for l in range(depth):
        kind, j = l % 3, l // 3
        mod = mods[l]
        if kind == 0:
            q, k, v = _qkv_project(lay, xs, mod, norm1_g[l], a_w_qkv[j], rope, a_q_gain[j], a_k_gain[j], seg_mean,
                                   n_kv_heads=4, qk_norm=True)
            o = _attention(lay, q, k, v, no_sink, n_kv_heads=4, window=False)
            w_o = a_w_o[j]
        elif kind == 1:
            p = (b_mu[j], b_w_r[j], b_w_k[j], b_w_v[j], b_decay_w0[j], b_decay_w1[j], b_decay_w2[j],
                 b_iclr_a0[j], b_iclr_a1[j], b_iclr_a2[j], b_gate_g1[j], b_gate_g2[j], b_k_k[j], b_k_a[j])
            r, v, kk, gate, lw, kd, b = _rwkv_features(lay, xs, mod, norm1_g[l], p, seg_sum)
            y2 = _rwkv_scan(r, v, kk, lw, kd, b, n_batch=n_batch, n_ctx=n_ctx)
            o = _rwkv_out(lay, y2, r, v, gate, kd, b_r_k[j], b_ln_w[j], b_ln_b[j], seg_mean)
            w_o = b_w_o[j]
        else:
            q, k, v = _qkv_project(lay, xs, mod, norm1_g[l], c_w_qkv[j], rope, no_gain, no_gain, seg_mean,
                                   n_kv_heads=2, qk_norm=False)
            o = _attention(lay, q, k, v, c_sink[j], n_kv_heads=2, window=True)
            w_o = c_w_o[j]
        w_route = jnp.zeros((d, LANES), F32).at[:, :N_GROUPS].set(moe_w_group[l]) \
            .at[:, N_GROUPS:N_GROUPS + N_EXPERTS].set(moe_w_expert[l])
        b_route = jnp.zeros((1, LANES), F32).at[0, :N_GROUPS].set(moe_b_group[l]) \
            .at[0, N_GROUPS:N_GROUPS + N_EXPERTS].set(moe_b_expert[l])
        xn, h2, route, counts = _post_mixer(lay, o, w_o, xs, mod, norm2_g[l], w_route, b_route)
        xs = _moe(lay, xn, h2, route, counts, mod, w1_all, w3_all, w2_all, l)
    return _final_norm(lay, xs, final_g)
```

```python
import functools
import math

import jax
import jax.numpy as jnp
from jax import lax
from jax.experimental import pallas as pl
from jax.experimental.pallas import tpu as pltpu

F32 = jnp.float32
BF16 = jnp.bfloat16
HI = lax.Precision.HIGHEST

HEAD = 64
LANES = 128
TM = 256
ATT_TQ = 128
ATT_TK = 2048
WINDOW = 128
SCAN_CHUNK = 64
MOE_ROWS = 256
N_GROUPS = 4
EXPERTS_PER_GROUP = 8
N_EXPERTS = N_GROUPS * EXPERTS_PER_GROUP
ROPE_THETA = 10000.0
NEG_INF = -1e30
Q_SCALE = HEAD ** -0.5
NORM_EPS = 1e-6
LN_X_EPS = 64e-5
VMEM_LIMIT = 56 * 1024 * 1024


def _params(n_axes):
    return pltpu.CompilerParams(dimension_semantics=("arbitrary",) * n_axes, vmem_limit_bytes=VMEM_LIMIT)


def _dot(a, b, precision=None):
    return jnp.dot(a, b, preferred_element_type=F32, precision=precision)


def _dot_nt(a, b, precision=None):
    return lax.dot_general(a, b, (((1,), (1,)), ((), ())), preferred_element_type=F32, precision=precision)


def _dot_tn(a, b, precision=None):
    return lax.dot_general(a, b, (((0,), (0,)), ((), ())), preferred_element_type=F32, precision=precision)


def _sigmoid(z):
    return 1.0 / (1.0 + jnp.exp(-z))


def _modulate(x, g, shift, scale):
    y = x * lax.rsqrt(jnp.mean(x * x, axis=-1, keepdims=True) + NORM_EPS)
    return (y * g) * (1.0 + scale) + shift


def _seg_apply(t, seg):
    seg2 = jnp.concatenate([seg, seg], axis=0).astype(BF16)
    t_hi = t.astype(BF16)
    t_lo = (t - t_hi.astype(F32)).astype(BF16)
    n = t.shape[1] // LANES
    tiles = [jnp.concatenate([t_hi[:, i * LANES:(i + 1) * LANES], t_lo[:, i * LANES:(i + 1) * LANES]], axis=1)
             for i in range(n)]
    return jnp.concatenate([_dot(tile, seg2) for tile in tiles], axis=1)


def _mod_kernel(c_ref, w_ref, b_ref, o_ref):
    cond = c_ref[...]
    o_ref[0] = _dot(cond * _sigmoid(cond), w_ref[0], HI) + b_ref[0]


def _mod_vectors(cond, mod_w, mod_b):
    depth, d, n = mod_w.shape
    tn = 1536
    return pl.pallas_call(
        _mod_kernel,
        grid=(depth, n // tn),
        in_specs=[pl.BlockSpec(cond.shape, lambda l, j: (0, 0)),
                  pl.BlockSpec((1, d, tn), lambda l, j: (l, 0, j)),
                  pl.BlockSpec((1, 1, tn), lambda l, j: (l, 0, j))],
        out_specs=pl.BlockSpec((1, cond.shape[0], tn), lambda l, j: (l, 0, j)),
        out_shape=jax.ShapeDtypeStruct((depth, cond.shape[0], n), F32),
        compiler_params=_params(2),
        name="mod_vectors",
    )(cond, mod_w, mod_b.reshape(depth, 1, n))


class _Layout:
    def __init__(self, n_batch, n_ctx, seq):
        assert n_ctx == TM and seq % TM == 0
        self.n_batch, self.n_ctx, self.seq = n_batch, n_ctx, seq
        self.per_batch = n_ctx + seq
        self.blocks_per_batch = self.per_batch // TM
        self.n_blocks = n_batch * self.blocks_per_batch
        self.t_all = n_batch * self.per_batch

    def mod_index(self, j):
        jb = j % self.blocks_per_batch
        return jnp.where(jb == 0, self.n_batch, j // self.blocks_per_batch)

    def mod_spec(self, d):
        return pl.BlockSpec((1, 6, d), lambda j: (self.mod_index(j), 0, 0))

    def row_spec(self, width):
        return pl.BlockSpec((TM, width), lambda j: (j, 0))

    def pos_spec(self, width):
        return pl.BlockSpec((TM, width), lambda j: (j % self.blocks_per_batch, 0))


def _full(shape):
    return pl.BlockSpec(shape, lambda *_: (0,) * len(shape))


def _qkv_kernel(x_ref, mod_ref, g_ref, w_ref, cos_ref, sina_ref, sinb_ref, qg_ref, kg_ref, seg_ref,
                q_ref, k_ref, v_ref, *, n_q, n_kv, qk_norm):
    h = _modulate(x_ref[...], g_ref[...], mod_ref[0, 0:1], mod_ref[0, 1:2])
    y = _dot(h.astype(BF16), w_ref[...])
    cos, sina, sinb = cos_ref[...], sina_ref[...], sinb_ref[...]
    seg = seg_ref[...]
    lane = lax.broadcasted_iota(jnp.int32, (TM, LANES), 1)
    low = lane < HEAD
    ones_col = jnp.where(lane == HEAD, 1.0, 0.0)

    def head_pair(t, gain):
        if qk_norm:
            t = t * lax.rsqrt(_seg_apply(t * t, seg) + NORM_EPS) * gain
        return t * cos + pltpu.roll(t, LANES - 16, 1) * sina + pltpu.roll(t, 16, 1) * sinb

    def both_heads(t):
        return t, pltpu.roll(t, HEAD, 1)

    for i in range(n_q // LANES):
        t = head_pair(y[:, i * LANES:(i + 1) * LANES], qg_ref[...]) * Q_SCALE
        for u, th in enumerate(both_heads(t)):
            q_ref[:, (2 * i + u) * LANES:(2 * i + u + 1) * LANES] = th.astype(BF16)
    for i in range(n_kv // LANES):
        t = head_pair(y[:, n_q + i * LANES:n_q + (i + 1) * LANES], kg_ref[...])
        for u, th in enumerate(both_heads(t)):
            k_ref[:, (2 * i + u) * LANES:(2 * i + u + 1) * LANES] = jnp.where(low, th, 0.0).astype(BF16)
        t = y[:, n_q + n_kv + i * LANES:n_q + n_kv + (i + 1) * LANES]
        for u, th in enumerate(both_heads(t)):
            v_ref[:, (2 * i + u) * LANES:(2 * i + u + 1) * LANES] = jnp.where(low, th, ones_col).astype(BF16)


def _qkv_project(lay, x, mod, g, w, rope, q_gain, k_gain, seg_mean, *, n_kv_heads, qk_norm):
    d = x.shape[1]
    n_q, n_kv = d, n_kv_heads * HEAD
    cos, sina, sinb = rope
    pair = lambda v: jnp.tile(v.reshape(1, HEAD), (1, LANES // HEAD))
    return pl.pallas_call(
        functools.partial(_qkv_kernel, n_q=n_q, n_kv=n_kv, qk_norm=qk_norm),
        grid=(lay.n_blocks,),
        in_specs=[lay.row_spec(d), lay.mod_spec(d), _full((1, d)), _full(w.shape),
                  lay.pos_spec(LANES), lay.pos_spec(LANES), lay.pos_spec(LANES),
                  _full((1, LANES)), _full((1, LANES)), _full((LANES, LANES))],
        out_specs=[lay.row_spec(2 * n_q), lay.row_spec(2 * n_kv), lay.row_spec(2 * n_kv)],
        out_shape=[jax.ShapeDtypeStruct((lay.t_all, 2 * n_q), BF16),
                   jax.ShapeDtypeStruct((lay.t_all, 2 * n_kv), BF16),
                   jax.ShapeDtypeStruct((lay.t_all, 2 * n_kv), BF16)],
        compiler_params=_params(1),
        name="qkv_project",
    )(x, mod, g.reshape(1, d), w.astype(BF16), cos, sina, sinb, pair(q_gain), pair(k_gain), seg_mean)


def _rope_tables(lay):
    rows = lay.seq // 64
    grid_w = 64
    row = jnp.repeat(jnp.arange(rows, dtype=F32), grid_w)
    col = jnp.tile(jnp.arange(grid_w, dtype=F32), rows)
    axis_dim = HEAD // 2
    inv_freq = ROPE_THETA ** (-jnp.arange(0, axis_dim, 2, dtype=F32) / axis_dim)
    a_row = row[:, None] * inv_freq
    a_col = col[:, None] * inv_freq
    ang = jnp.concatenate([a_row, a_row, a_col, a_col], axis=-1)
    cos = jnp.concatenate([jnp.ones((lay.n_ctx, HEAD), F32), jnp.cos(ang)], axis=0)
    sin = jnp.concatenate([jnp.zeros((lay.n_ctx, HEAD), F32), jnp.sin(ang)], axis=0)
    first = (jnp.arange(HEAD) % 32) < 16
    sina = jnp.where(first, -sin, 0.0)
    sinb = jnp.where(first, 0.0, sin)
    wide = lambda t: jnp.tile(t, (1, LANES // HEAD))
    return wide(cos), wide(sina), wide(sinb)


def _attn_kernel(sink_ref, q_ref, k_ref, v_ref, o_ref, *, n_kv, grp, window, n_ctx, seq):
    tq = ATT_TQ
    j = pl.program_id(1)
    rows = grp * tq
    ctx_blocks = n_ctx // tq
    gs = range(n_kv)
    tile = lambda ref, r0, g: ref[r0, g * LANES:(g + 1) * LANES]
    qg = [jnp.concatenate([tile(q_ref, slice(None), g * grp + u) for u in range(grp)], axis=0) for g in gs]

    def step(carry, ks, vs, keep=None):
        s = [_dot_nt(qg[g], ks[g]) for g in gs]
        if keep is not None:
            s = [jnp.where(keep, s[g], NEG_INF) for g in gs]
        m_new = [jnp.maximum(carry[g][0], jnp.max(s[g], axis=1, keepdims=True)) for g in gs]
        p = [jnp.exp(s[g] - m_new[g]).astype(BF16) for g in gs]
        alpha = [jnp.exp(carry[g][0] - m_new[g]) for g in gs]
        pv = [_dot(p[g], vs[g]) for g in gs]
        return tuple((m_new[g], alpha[g] * carry[g][1] + pv[g]) for g in gs)

    if window:
        lane = lax.broadcasted_iota(jnp.int32, (rows, LANES), 1)
        acc0 = jnp.where(lane == HEAD, 1.0, 0.0)
        carry = tuple((jnp.concatenate([jnp.full((tq, 1), sink_ref[g * grp + u], F32) for u in range(grp)],
                                       axis=0), acc0) for g in gs)
        tk = WINDOW
        rr = lax.broadcasted_iota(jnp.int32, (rows, tk), 0) & (tq - 1)
        cc = lax.broadcasted_iota(jnp.int32, (rows, tk), 1)
        j_lat = j - ctx_blocks
        n_lat = seq // tq
        never = 4 * tk
        for cb in range(n_ctx // tk):
            r0 = slice(cb * tk, (cb + 1) * tk)
            carry = step(carry, [tile(k_ref, r0, g) for g in gs], [tile(v_ref, r0, g) for g in gs])
        for off in (-1, 0, 1):
            blk = j_lat + off
            valid = jnp.logical_and(j_lat >= 0, jnp.logical_and(blk >= 0, blk < n_lat))
            r0 = pl.ds(pl.multiple_of(n_ctx + jnp.clip(blk, 0, n_lat - 1) * tk, tk), tk)
            bar = jnp.where(valid, 0, never)
            keep = {-1: cc >= rr + bar, 0: cc >= bar, 1: cc + bar <= rr}[off]
            carry = step(carry, [tile(k_ref, r0, g) for g in gs], [tile(v_ref, r0, g) for g in gs], keep)
    else:
        tk = min(ATT_TK, seq)
        assert seq % tk == 0
        carry = tuple((jnp.full((rows, 1), NEG_INF, F32), jnp.zeros((rows, LANES), F32)) for g in gs)
        ctx_rows = slice(0, n_ctx)
        carry = step(carry, [tile(k_ref, ctx_rows, g) for g in gs], [tile(v_ref, ctx_rows, g) for g in gs])
        n_tiles = jnp.where(j < ctx_blocks, 0, seq // tk)

        def body(t, c):
            r0 = pl.ds(pl.multiple_of(n_ctx + t * tk, math.gcd(n_ctx, tk)), tk)
            return step(c, [tile(k_ref, r0, g) for g in gs], [tile(v_ref, r0, g) for g in gs])

        carry = lax.fori_loop(0, n_tiles, body, carry)
    for g in gs:
        acc = carry[g][1]
        out = acc[:, :HEAD] / acc[:, HEAD:HEAD + 1]
        for u in range(grp):
            hd = g * grp + u
            o_ref[:, hd * HEAD:(hd + 1) * HEAD] = out[u * tq:(u + 1) * tq].astype(BF16)


def _attention(lay, q, k, v, sink, *, n_kv_heads, window):
    d = q.shape[1] // 2
    n_kv = n_kv_heads * LANES
    per_q = lay.per_batch // ATT_TQ
    kv_spec = pl.BlockSpec((lay.per_batch, n_kv), lambda b, j, *_: (b, 0), pipeline_mode=pl.Buffered(1))
    q_spec = pl.BlockSpec((ATT_TQ, 2 * d), lambda b, j, *_: (b * per_q + j, 0))
    o_spec = pl.BlockSpec((ATT_TQ, d), lambda b, j, *_: (b * per_q + j, 0))
    return pl.pallas_call(
        functools.partial(_attn_kernel, n_kv=n_kv_heads, grp=d // HEAD // n_kv_heads, window=window,
                          n_ctx=lay.n_ctx, seq=lay.seq),
        grid_spec=pltpu.PrefetchScalarGridSpec(
            num_scalar_prefetch=1, grid=(lay.n_batch, per_q),
            in_specs=[q_spec, kv_spec, kv_spec], out_specs=o_spec),
        out_shape=jax.ShapeDtypeStruct((lay.t_all, d), BF16),
        compiler_params=_params(2),
        name="window_attention" if window else "dense_attention",
    )(sink, q, k, v)


def _route(logits):
    lane = lax.broadcasted_iota(jnp.int32, logits.shape, 1)
    far = 4 * LANES
    is_g = lane < N_GROUPS
    gmax = jnp.max(jnp.where(is_g, logits, NEG_INF), axis=1, keepdims=True)
    gidx = jnp.min(jnp.where(jnp.logical_and(is_g, logits == gmax), lane, far), axis=1, keepdims=True)
    gsum = jnp.sum(jnp.where(is_g, jnp.exp(logits - gmax), 0.0), axis=1, keepdims=True)
    g_w = 1.0 / gsum
    is_e = jnp.logical_and(lane >= N_GROUPS, ((lane - N_GROUPS) >> 3) == gidx)
    emax = jnp.max(jnp.where(is_e, logits, NEG_INF), axis=1, keepdims=True)
    ee = jnp.where(is_e, jnp.exp(logits - emax), 0.0)
    p = ee / jnp.sum(ee, axis=1, keepdims=True)
    p_in = jnp.where(is_e, p, -1.0)
    p1 = jnp.max(p_in, axis=1, keepdims=True)
    i1 = jnp.min(jnp.where(p_in == p1, lane, far), axis=1, keepdims=True)
    p_rest = jnp.where(lane == i1, -1.0, p_in)
    p2 = jnp.max(p_rest, axis=1, keepdims=True)
    i2 = jnp.min(jnp.where(p_rest == p2, lane, far), axis=1, keepdims=True)
    tot = p1 + p2
    w1 = g_w * p1 / tot
    w2 = g_w * p2 / tot
    e1 = (i1 - N_GROUPS).astype(F32)
    e2 = (i2 - N_GROUPS).astype(F32)
    picks = (jnp.where(lane == i1, 1.0, 0.0), jnp.where(lane == i2, 1.0, 0.0))
    return (e1, e2, w1, w2), picks


def _post_kernel(o_ref, wo_ref, x_ref, mod_ref, g2_ref, wr_hi_ref, wr_lo_ref, br_ref,
                 xn_ref, h2_ref, route_ref, count_ref, seen_ref):
    @pl.when(pl.program_id(0) == 0)
    def _():
        seen_ref[...] = jnp.zeros_like(seen_ref)

    y = _dot(o_ref[...], wo_ref[...])
    xn = x_ref[...] + mod_ref[0, 2:3] * y
    xn_ref[...] = xn
    h2 = _modulate(xn, g2_ref[...], mod_ref[0, 3:4], mod_ref[0, 4:5])
    h2_ref[...] = h2
    h_hi = _bf(h2)
    h_lo = _bf(h2 - h_hi.astype(F32))
    logits = _dot(h_hi, wr_hi_ref[...]) + (_dot(h_hi, wr_lo_ref[...]) + _dot(h_lo, wr_hi_ref[...]))
    cols, (pick1, pick2) = _route(logits + br_ref[...])
    both = pick1 + pick2
    row = lax.broadcasted_iota(jnp.int32, (TM, TM), 0)
    col = lax.broadcasted_iota(jnp.int32, (TM, TM), 1)
    earlier = jnp.where(col < row, 1.0, 0.0).astype(BF16)
    before = _dot(earlier, both.astype(BF16)) + seen_ref[0:1]
    ranks = (jnp.sum(before * pick1, axis=1, keepdims=True), jnp.sum(before * pick2, axis=1, keepdims=True))
    seen = seen_ref[0:1] + jnp.sum(both, axis=0, keepdims=True)
    seen_ref[...] = jnp.broadcast_to(seen, seen_ref.shape)
    count_ref[...] = jnp.broadcast_to(seen, count_ref.shape)
    lane = lax.broadcasted_iota(jnp.int32, (TM, LANES), 1)
    out = jnp.zeros((TM, LANES), F32)
    for i, c in enumerate(cols + ranks):
        out = jnp.where(lane == i, c, out)
    route_ref[...] = out


def _post_mixer(lay, o, w_o, x, mod, g2, w_route, b_route):
    d = x.shape[1]
    w_hi = w_route.astype(BF16)
    return pl.pallas_call(
        _post_kernel,
        grid=(lay.n_blocks,),
        in_specs=[lay.row_spec(d), _full((d, d)), lay.row_spec(d), lay.mod_spec(d), _full((1, d)),
                  _full((d, LANES)), _full((d, LANES)), _full((1, LANES))],
        out_specs=[lay.row_spec(d), lay.row_spec(d), lay.row_spec(LANES), _full((8, LANES))],
        out_shape=[jax.ShapeDtypeStruct((lay.t_all, d), F32), jax.ShapeDtypeStruct((lay.t_all, d), F32),
                   jax.ShapeDtypeStruct((lay.t_all, LANES), F32), jax.ShapeDtypeStruct((8, LANES), F32)],
        scratch_shapes=[pltpu.VMEM((8, LANES), F32)],
        compiler_params=_params(1),
        name="post_mixer",
    )(o, w_o.astype(BF16), x, mod, g2.reshape(1, d), w_hi, (w_route - w_hi.astype(F32)).astype(BF16), b_route)


def _dispatch_kernel(dest_ref, h_ref, xs_in, xs_hbm, sem):
    del xs_in

    def row_copy(r, k):
        return pltpu.make_async_copy(h_ref.at[pl.ds(r, 1)], xs_hbm.at[pl.ds(dest_ref[0, k, r], 1)], sem)

    def start(r, carry):
        row_copy(r, 0).start()
        row_copy(r, 1).start()
        return carry

    def wait(r, carry):
        row_copy(r, 0).wait()
        row_copy(r, 1).wait()
        return carry

    lax.fori_loop(0, TM, start, 0, unroll=8)
    lax.fori_loop(0, TM, wait, 0, unroll=8)


def _dispatch(lay, h2, dest_blocks, n_rows):
    d = h2.shape[1]
    return pl.pallas_call(
        _dispatch_kernel,
        grid=(lay.n_blocks,),
        in_specs=[pl.BlockSpec((1, 2, TM), lambda j: (j, 0, 0), memory_space=pltpu.SMEM), lay.row_spec(d),
                  pl.BlockSpec(memory_space=pl.ANY)],
        out_specs=pl.BlockSpec(memory_space=pl.ANY),
        out_shape=jax.ShapeDtypeStruct((n_rows, d), F32),
        scratch_shapes=[pltpu.SemaphoreType.DMA(())],
        input_output_aliases={2: 0},
        compiler_params=_params(1),
        name="moe_dispatch",
    )(dest_blocks, h2, jnp.zeros((n_rows, d), F32))


def _expert_kernel(blk_ref, used_ref, x_ref, w1_ref, w3_ref, w2_ref, out_ref):
    del blk_ref
    i = pl.program_id(0)

    @pl.when(i < used_ref[0])
    def _():
        xb = x_ref[...].astype(BF16)
        a = _dot(xb, w1_ref[0].astype(BF16))
        hid = (a * _sigmoid(a)) * _dot(xb, w3_ref[0].astype(BF16))
        out_ref[...] = _dot(hid.astype(BF16), w2_ref[0].astype(BF16))

    @pl.when(i >= used_ref[0])
    def _():
        out_ref[...] = jnp.zeros_like(out_ref)


def _expert_blocks(xs, blk_expert, n_used, w1, w3, w2, layer):
    n_rows, d = xs.shape
    n_blocks = n_rows // MOE_ROWS
    de = w1.shape[2]
    base = layer * N_EXPERTS
    return pl.pallas_call(
        _expert_kernel,
        grid_spec=pltpu.PrefetchScalarGridSpec(
            num_scalar_prefetch=2, grid=(n_blocks,),
            in_specs=[pl.BlockSpec((MOE_ROWS, d), lambda i, blk, used: (i, 0)),
                      pl.BlockSpec((1, d, de), lambda i, blk, used: (base + blk[i], 0, 0)),
                      pl.BlockSpec((1, d, de), lambda i, blk, used: (base + blk[i], 0, 0)),
                      pl.BlockSpec((1, de, d), lambda i, blk, used: (base + blk[i], 0, 0))],
            out_specs=pl.BlockSpec((MOE_ROWS, d), lambda i, blk, used: (i, 0))),
        out_shape=jax.ShapeDtypeStruct((n_rows, d), F32),
        compiler_params=_params(1),
        name="expert_blocks",
    )(blk_expert, n_used, xs, w1, w3, w2)


def _combine_kernel(pos_ref, ys_hbm, x_ref, mod_ref, route_ref, o_ref, buf, sem):
    def row_copy(r, k):
        return pltpu.make_async_copy(ys_hbm.at[pl.ds(pos_ref[0, k, r], 1)], buf.at[k, pl.ds(r, 1)], sem)

    def start(r, carry):
        row_copy(r, 0).start()
        row_copy(r, 1).start()
        return carry

    def wait(r, carry):
        row_copy(r, 0).wait()
        row_copy(r, 1).wait()
        return carry

    lax.fori_loop(0, TM, start, 0, unroll=8)
    lax.fori_loop(0, TM, wait, 0, unroll=8)
    route = route_ref[...]
    f = buf[0] * route[:, 2:3] + buf[1] * route[:, 3:4]
    o_ref[...] = x_ref[...] + mod_ref[0, 5:6] * f


def _combine(lay, ys, pos, x, mod, route):
    d = x.shape[1]
    return pl.pallas_call(
        _combine_kernel,
        grid=(lay.n_blocks,),
        in_specs=[pl.BlockSpec((1, 2, TM), lambda j: (j, 0, 0), memory_space=pltpu.SMEM),
                  pl.BlockSpec(memory_space=pl.ANY), lay.row_spec(d), lay.mod_spec(d), lay.row_spec(LANES)],
        out_specs=lay.row_spec(d),
        out_shape=jax.ShapeDtypeStruct((lay.t_all, d), F32),
        scratch_shapes=[pltpu.VMEM((2, TM, d), F32), pltpu.SemaphoreType.DMA(())],
        compiler_params=_params(1),
        name="moe_combine",
    )(pos, ys, x, mod, route)


def _dispatch_plan(route, counts):
    n_tok = route.shape[0]
    n_blocks = -(-(n_tok * 2) // MOE_ROWS) + N_EXPERTS
    counts = counts[0, N_GROUPS:N_GROUPS + N_EXPERTS].astype(jnp.int32)
    padded = (counts + MOE_ROWS - 1) // MOE_ROWS * MOE_ROWS
    p_end = jnp.cumsum(padded)
    first_slot = p_end - padded
    expert = route[:, 0:2].astype(jnp.int32)
    onehot = expert[:, :, None] == jnp.arange(N_EXPERTS, dtype=jnp.int32)
    dest = jnp.sum(jnp.where(onehot, first_slot, 0), axis=-1) + route[:, 4:6].astype(jnp.int32)
    block_start = jnp.arange(n_blocks, dtype=jnp.int32) * MOE_ROWS
    blk_expert = jnp.minimum(jnp.sum(p_end[None, :] <= block_start[:, None], axis=1), N_EXPERTS - 1).astype(jnp.int32)
    n_used = (p_end[-1:] // MOE_ROWS).astype(jnp.int32)
    return dest, blk_expert, n_used, n_blocks


def _moe(lay, xn, h2, route, counts, mod, w1, w3, w2, layer):
    dest, blk_expert, n_used, n_blocks = _dispatch_plan(route, counts)
    dest_blocks = dest.reshape(lay.n_blocks, TM, 2).transpose(0, 2, 1)
    xs = _dispatch(lay, h2, dest_blocks, n_blocks * MOE_ROWS)
    ys = _expert_blocks(xs, blk_expert, n_used, w1, w3, w2, layer)
    return _combine(lay, ys, dest_blocks, xn, mod, route)


def _rwkv_feat_kernel(x_ref, xp_ref, xq_ref, mod_ref, g_ref, mu_ref, wr_ref, wk_ref, wv_ref, w0_ref, w1_ref, w2_ref,
                      a0_ref, a1_ref, a2_ref, g1_ref, g2_ref, kk_ref, ka_ref, seg_ref,
                      r_ref, v_ref, kn_ref, gate_ref, lw_ref, kd_ref, b_ref, *, blocks_per_batch):
    jb = pl.program_id(0) % blocks_per_batch
    shift, scale, g = mod_ref[0, 0:1], mod_ref[0, 1:2], g_ref[...]
    h = _modulate(x_ref[...], g, shift, scale)
    has_prev = jnp.where(jb > 1, 1.0, 0.0)
    has_next = jnp.where(jnp.logical_and(jb > 0, jb < blocks_per_batch - 1), 1.0, 0.0)
    h_before = _modulate(xp_ref[...], g, shift, scale)[7:8] * has_prev
    h_after = _modulate(xq_ref[...], g, shift, scale)[0:1] * has_next
    rows = lax.broadcasted_iota(jnp.int32, h.shape, 0)
    prev = jnp.where(rows == 0, h_before, pltpu.roll(h, 1, 0))
    nxt = jnp.where(rows == TM - 1, h_after, pltpu.roll(h, TM - 1, 0))
    xx = 0.5 * (prev + nxt) - h
    mix = lambda i: (h + xx * mu_ref[i:i + 1]).astype(BF16)
    xr, xw, xk, xv, xa, xg = [mix(i) for i in range(6)]
    r = _dot(xr, wr_ref[...])
    k = _dot(xk, wk_ref[...])
    r_ref[...] = r
    v_ref[...] = _dot(xv, wv_ref[...])
    gate_ref[...] = _dot(_sigmoid(_dot(xg, g1_ref[...])).astype(BF16), g2_ref[...])
    kk = k * kk_ref[...]
    norm = jnp.sqrt(_seg_apply(kk * kk, seg_ref[...]))
    kk = kk / jnp.maximum(norm, 1e-12)
    kn_ref[...] = kk
    for n in range(2):
        w_log = w0_ref[n:n + 1] + _dot(jnp.tanh(_dot(xw, w1_ref[n])).astype(BF16), w2_ref[n])
        z = -w_log
        softplus = jnp.maximum(z, 0.0) + jnp.log(1.0 + jnp.exp(-jnp.abs(z)))
        lw_ref[n] = -jnp.exp(-softplus - 0.5)
        iclr = _sigmoid(a0_ref[n:n + 1] + _dot(_dot(xa, a1_ref[n]).astype(BF16), a2_ref[n]))
        kd_ref[n] = k * (1.0 + (iclr - 1.0) * ka_ref[...])
        b_ref[n] = kk * iclr


def _rwkv_features(lay, x, mod, g, p, seg_sum):
    (mu, w_r, w_k, w_v, w0, w1, w2, a0, a1, a2, g1, g2, k_k, k_a) = p
    d = x.shape[1]
    row = lay.row_spec(d)
    sub = TM // 8
    last = lay.t_all // 8 - 1
    before = pl.BlockSpec((8, d), lambda j: (jnp.maximum(j * sub - 1, 0), 0))
    after = pl.BlockSpec((8, d), lambda j: (jnp.minimum((j + 1) * sub, last), 0))
    both = pl.BlockSpec((2, TM, d), lambda j: (0, j, 0))
    bf = lambda t: t.astype(BF16)
    args = (x, x, x, mod, g.reshape(1, d), mu, bf(w_r), bf(w_k), bf(w_v), w0, bf(w1), bf(w2), a0, bf(a1), bf(a2),
            bf(g1), bf(g2), k_k.reshape(1, d), k_a.reshape(1, d), seg_sum)
    in_specs = [row, before, after, lay.mod_spec(d)] + [_full(a.shape) for a in args[4:]]
    shape = jax.ShapeDtypeStruct((lay.t_all, d), F32)
    shape2 = jax.ShapeDtypeStruct((2, lay.t_all, d), F32)
    return pl.pallas_call(
        functools.partial(_rwkv_feat_kernel, blocks_per_batch=lay.blocks_per_batch),
        grid=(lay.n_blocks,),
        in_specs=in_specs,
        out_specs=[row, row, row, row, both, both, both],
        out_shape=[shape, shape, shape, shape, shape2, shape2, shape2],
        compiler_params=_params(1),
        name="rwkv_features",
    )(*args)


def _scan_kernel(r_ref, v_ref, kk_ref, lw_ref, kd_ref, b_ref, y_ref, s_ref, *, n_heads):
    c = SCAN_CHUNK
    direction = pl.program_id(0)
    step = pl.program_id(2)

    @pl.when(step == 0)
    def _():
        s_ref[...] = jnp.zeros_like(s_ref)

    row = lax.broadcasted_iota(jnp.int32, (c, c), 0)
    col = lax.broadcasted_iota(jnp.int32, (c, c), 1)
    sign = 1 - 2 * direction
    lag = (row - col) * sign
    incl = lag >= 0

    lw = lw_ref[0]
    lg = _dot(incl.astype(F32), lw, HI)
    lg_tot = jnp.sum(lw, axis=0, keepdims=True)
    g_inv = jnp.exp(-lg)
    g_rem = jnp.exp(lg_tot - lg)
    g_tot = jnp.exp(lg_tot)
    v_all = v_ref[...]
    even = (lax.broadcasted_iota(jnp.int32, lw.shape, 1) & HEAD) == 0
    halves = lambda t: (_bf(jnp.where(even, t, 0.0)), _bf(jnp.where(even, 0.0, t)))
    ar_all = _bf(jnp.concatenate([-kk_ref[...] * jnp.exp(lg - lw), r_ref[...] * jnp.exp(lg)], axis=0))
    b_t, k_t = halves(b_ref[0] * g_inv), halves(kd_ref[0] * g_inv)
    b_h, k_h = halves(b_ref[0] * g_rem), halves(kd_ref[0] * g_rem)

    lag2 = (lax.broadcasted_iota(jnp.int32, (c, LANES), 0)
            - (lax.broadcasted_iota(jnp.int32, (c, LANES), 1) & (HEAD - 1))) * sign
    strict2 = lag2 > 0
    incl2 = lag2 >= 0
    zeros_v = jnp.zeros((c, HEAD), BF16)
    zeros_l = jnp.zeros((c, LANES), BF16)

    hs = range(n_heads)
    pair = [slice(h // 2 * LANES, (h // 2 + 1) * LANES) for h in hs]
    ar = [ar_all[:, pair[h]] for h in hs]
    bk = [jnp.concatenate([b_t[h % 2][:, pair[h]], k_t[h % 2][:, pair[h]], b_t[h % 2][:, pair[h]],
                           b_t[h % 2][:, pair[h]]], axis=0) for h in hs]
    bkh = [jnp.concatenate([b_h[h % 2][:, pair[h]], k_h[h % 2][:, pair[h]]], axis=0) for h in hs]
    v_h = [v_all[:, h * HEAD:(h + 1) * HEAD] for h in hs]
    s0 = [s_ref[h] for h in hs]
    p = [_dot_nt(ar[h], bk[h]) for h in hs]
    ars = [_dot_nt(ar[h], _bf(s0[h])) for h in hs]
    x = [ars[h][:c] + _dot(_bf(jnp.where(strict2, p[h][:c, :LANES], 0.0)),
                           jnp.concatenate([zeros_v, _bf(v_h[h])], axis=0)) for h in hs]
    pw = [_bf(jnp.where(strict2, p[h][:c, LANES:], 0.0)) for h in hs]
    x = [x[h] + _apply_split(pw[h], x[h]) for h in hs]
    for _ in range(5):
        pw = [_bf(_dot(pw[h], jnp.concatenate([pw[h], zeros_l], axis=0))) for h in hs]
        x = [x[h] + _apply_split(pw[h], x[h]) for h in hs]
    uv = [_bf(jnp.concatenate([x[h], v_h[h]], axis=0)) for h in hs]
    y_ref[0] = jnp.concatenate(
        [ars[h][c:] + _dot(_bf(jnp.where(incl2, p[h][c:, :LANES], 0.0)), uv[h]) for h in hs], axis=1)
    states = [s0[h] * g_tot[:, pair[h]] + _dot_tn(uv[h], bkh[h]) for h in hs]
    for h in hs:
        s_ref[h] = states[h]


def _bf(t):
    return t.astype(BF16)


def _apply_split(l_dup, x):
    x_hi = _bf(x)
    x_lo = _bf(x - x_hi.astype(F32))
    return _dot(l_dup, jnp.concatenate([x_hi, x_lo], axis=0))


def _rwkv_scan(r, v, kk, lw, kd, b, *, n_batch, n_ctx):
    t_all, d = r.shape
    c = SCAN_CHUNK
    per_batch = t_all // n_batch // c
    ctx_chunks = n_ctx // c
    n_heads = d // HEAD

    def chunk(dr, bi, i):
        bwd_idx = jnp.where(i < ctx_chunks, ctx_chunks - 1 - i, per_batch - 1 - (i - ctx_chunks))
        return bi * per_batch + jnp.where(dr == 0, i, bwd_idx)

    shared = pl.BlockSpec((c, d), lambda dr, bi, i: (chunk(dr, bi, i), 0))
    per_dir = pl.BlockSpec((1, c, d), lambda dr, bi, i: (dr, chunk(dr, bi, i), 0))
    return pl.pallas_call(
        functools.partial(_scan_kernel, n_heads=n_heads),
        grid=(2, n_batch, per_batch),
        in_specs=[shared, shared, shared, per_dir, per_dir, per_dir],
        out_specs=per_dir,
        out_shape=jax.ShapeDtypeStruct((2, t_all, d), F32),
        scratch_shapes=[pltpu.VMEM((n_heads, HEAD, LANES), F32)],
        compiler_params=_params(3),
        name="rwkv_scan",
    )(r, v, kk, lw, kd, b)


def _rwkv_out_kernel(y_ref, r_ref, v_ref, gate_ref, kd_ref, rk_ref, lnw_ref, lnb_ref, seg_ref, o_ref):
    seg = seg_ref[...]
    y = y_ref[0] + y_ref[1]
    mean = _seg_apply(y, seg)
    cen = y - mean
    var = _seg_apply(cen * cen, seg)
    yn = cen * lax.rsqrt(var + LN_X_EPS) * lnw_ref[...] + lnb_ref[...]
    bonus = _seg_apply(r_ref[...] * (kd_ref[0] + kd_ref[1]) * rk_ref[...], seg) * float(HEAD) * v_ref[...]
    o_ref[...] = ((yn + bonus) * gate_ref[...]).astype(BF16)


def _rwkv_out(lay, y2, r, v, gate, kd, r_k, ln_w, ln_b, seg_mean):
    d = r.shape[1]
    row = lay.row_spec(d)
    both = pl.BlockSpec((2, TM, d), lambda j: (0, j, 0))
    return pl.pallas_call(
        _rwkv_out_kernel,
        grid=(lay.n_blocks,),
        in_specs=[both, row, row, row, both, _full((1, d)), _full((1, d)), _full((1, d)), _full((LANES, LANES))],
        out_specs=row,
        out_shape=jax.ShapeDtypeStruct((lay.t_all, d), BF16),
        compiler_params=_params(1),
        name="rwkv_out",
    )(y2, r, v, gate, kd, r_k.reshape(1, d), ln_w.reshape(1, d), ln_b.reshape(1, d), seg_mean)


def _final_kernel(x_ref, g_ref, o_ref):
    x = x_ref[...]
    o_ref[0] = x * lax.rsqrt(jnp.mean(x * x, axis=-1, keepdims=True) + NORM_EPS) * g_ref[...]


def _final_norm(lay, x, g):
    d = x.shape[1]
    lat_blocks = lay.seq // TM
    return pl.pallas_call(
        _final_kernel,
        grid=(lay.n_batch, lat_blocks),
        in_specs=[pl.BlockSpec((TM, d), lambda b, j: (b * lay.blocks_per_batch + 1 + j, 0)), _full((1, d))],
        out_specs=pl.BlockSpec((1, TM, d), lambda b, j: (b, j, 0)),
        out_shape=jax.ShapeDtypeStruct((lay.n_batch, lay.seq, d), F32),
        compiler_params=_params(2),
        name="final_norm",
    )(x, g.reshape(1, d))


def kernel(x, c, ctx, c_ctx, mod_w, mod_b, norm1_g, norm2_g, a_w_qkv, a_w_o, a_q_gain, a_k_gain, b_mu, b_w_r, b_w_k, b_w_v, b_w_o, b_decay_w0, b_decay_w1, b_decay_w2, b_iclr_a0, b_iclr_a1, b_iclr_a2, b_gate_g1, b_gate_g2, b_k_k, b_k_a, b_r_k, b_ln_w, b_ln_b, c_w_qkv, c_w_o, c_sink, moe_w_group, moe_b_group, moe_w_expert, moe_b_expert, moe_w1, moe_w3, moe_w2, final_g):
    n_batch, seq, d = x.shape
    n_ctx = ctx.shape[1]
    depth = mod_w.shape[0]
    lay = _Layout(n_batch, n_ctx, seq)

    cond = jnp.zeros((8, d), F32).at[:n_batch].set(c).at[n_batch].set(c_ctx)
    mods = _mod_vectors(cond, mod_w, mod_b)[:, :n_batch + 1].reshape(depth, n_batch + 1, 6, d)
    xs = jnp.concatenate([ctx, x], axis=1).reshape(lay.t_all, d)

    blockdiag = (jnp.arange(LANES)[:, None] // HEAD == jnp.arange(LANES)[None, :] // HEAD).astype(F32)
    seg_sum, seg_mean = blockdiag, blockdiag / HEAD
    rope = _rope_tables(lay)
    no_gain = jnp.ones((HEAD,), F32)
    stacked = lambda w: w.reshape((-1,) + w.shape[2:])
    w1_all, w3_all, w2_all = stacked(moe_w1), stacked(moe_w3), stacked(moe_w2)
    no_sink = jnp.zeros((d // HEAD,), F32)

    for l in range(depth):
        kind, j = l % 3, l // 3
        mod = mods[l]
        if kind == 0:
            q, k, v = _qkv_project(lay, xs, mod, norm1_g[l], a_w_qkv[j], rope, a_q_gain[j], a_k_gain[j], seg_mean,
                                   n_kv_heads=4, qk_norm=True)
            o = _attention(lay, q, k, v, no_sink, n_kv_heads=4, window=False)
            w_o = a_w_o[j]
        elif kind == 1:
            p = (b_mu[j], b_w_r[j], b_w_k[j], b_w_v[j], b_decay_w0[j], b_decay_w1[j], b_decay_w2[j],
                 b_iclr_a0[j], b_iclr_a1[j], b_iclr_a2[j], b_gate_g1[j], b_gate_g2[j], b_k_k[j], b_k_a[j])
            r, v, kk, gate, lw, kd, b = _rwkv_features(lay, xs, mod, norm1_g[l], p, seg_sum)
            y2 = _rwkv_scan(r, v, kk, lw, kd, b, n_batch=n_batch, n_ctx=n_ctx)
            o = _rwkv_out(lay, y2, r, v, gate, kd, b_r_k[j], b_ln_w[j], b_ln_b[j], seg_mean)
            w_o = b_w_o[j]
        else:
            q, k, v = _qkv_project(lay, xs, mod, norm1_g[l], c_w_qkv[j], rope, no_gain, no_gain, seg_mean,
                                   n_kv_heads=2, qk_norm=False)
            o = _attention(lay, q, k, v, c_sink[j], n_kv_heads=2, window=True)
            w_o = c_w_o[j]
        w_route = jnp.zeros((d, LANES), F32).at[:, :N_GROUPS].set(moe_w_group[l]) \
            .at[:, N_GROUPS:N_GROUPS + N_EXPERTS].set(moe_w_expert[l])
        b_route = jnp.zeros((1, LANES), F32).at[0, :N_GROUPS].set(moe_b_group[l]) \
            .at[0, N_GROUPS:N_GROUPS + N_EXPERTS].set(moe_b_expert[l])
        xn, h2, route, counts = _post_mixer(lay, o, w_o, xs, mod, norm2_g[l], w_route, b_route)
        xs = _moe(lay, xn, h2, route, counts, mod, w1_all, w3_all, w2_all, l)
    return _final_norm(lay, xs, final_g)
```

```python
import functools
import math

import jax
import jax.numpy as jnp
from jax import lax
from jax.experimental import pallas as pl
from jax.experimental.pallas import tpu as pltpu

F32 = jnp.float32
BF16 = jnp.bfloat16
HI = lax.Precision.HIGHEST

HEAD = 64
LANES = 128
TM = 256
ATT_TQ = 128
ATT_TK = 2048
WINDOW = 128
SCAN_CHUNK = 64
MOE_ROWS = 256
N_GROUPS = 4
EXPERTS_PER_GROUP = 8
N_EXPERTS = N_GROUPS * EXPERTS_PER_GROUP
ROPE_THETA = 10000.0
NEG_INF = -1e30
Q_SCALE = HEAD ** -0.5
NORM_EPS = 1e-6
LN_X_EPS = 64e-5
VMEM_LIMIT = 56 * 1024 * 1024


def _params(n_axes):
    return pltpu.CompilerParams(dimension_semantics=("arbitrary",) * n_axes, vmem_limit_bytes=VMEM_LIMIT)


def _dot(a, b, precision=None):
    return jnp.dot(a, b, preferred_element_type=F32, precision=precision)


def _dot_nt(a, b, precision=None):
    return lax.dot_general(a, b, (((1,), (1,)), ((), ())), preferred_element_type=F32, precision=precision)


def _dot_tn(a, b, precision=None):
    return lax.dot_general(a, b, (((0,), (0,)), ((), ())), preferred_element_type=F32, precision=precision)


def _sigmoid(z):
    return 1.0 / (1.0 + jnp.exp(-z))


def _modulate(x, g, shift, scale):
    y = x * lax.rsqrt(jnp.mean(x * x, axis=-1, keepdims=True) + NORM_EPS)
    return (y * g) * (1.0 + scale) + shift


def _seg_apply(t, seg):
    seg2 = jnp.concatenate([seg, seg], axis=0).astype(BF16)
    t_hi = t.astype(BF16)
    t_lo = (t - t_hi.astype(F32)).astype(BF16)
    n = t.shape[1] // LANES
    tiles = [jnp.concatenate([t_hi[:, i * LANES:(i + 1) * LANES], t_lo[:, i * LANES:(i + 1) * LANES]], axis=1)
             for i in range(n)]
    return jnp.concatenate([_dot(tile, seg2) for tile in tiles], axis=1)


def _mod_kernel(c_ref, w_ref, b_ref, o_ref):
    cond = c_ref[...]
    o_ref[0] = _dot(cond * _sigmoid(cond), w_ref[0], HI) + b_ref[0]


def _mod_vectors(cond, mod_w, mod_b):
    depth, d, n = mod_w.shape
    tn = 1536
    return pl.pallas_call(
        _mod_kernel,
        grid=(depth, n // tn),
        in_specs=[pl.BlockSpec(cond.shape, lambda l, j: (0, 0)),
                  pl.BlockSpec((1, d, tn), lambda l, j: (l, 0, j)),
                  pl.BlockSpec((1, 1, tn), lambda l, j: (l, 0, j))],
        out_specs=pl.BlockSpec((1, cond.shape[0], tn), lambda l, j: (l, 0, j)),
        out_shape=jax.ShapeDtypeStruct((depth, cond.shape[0], n), F32),
        compiler_params=_params(2),
        name="mod_vectors",
    )(cond, mod_w, mod_b.reshape(depth, 1, n))


class _Layout:
    def __init__(self, n_batch, n_ctx, seq):
        assert n_ctx == TM and seq % TM == 0
        self.n_batch, self.n_ctx, self.seq = n_batch, n_ctx, seq
        self.per_batch = n_ctx + seq
        self.blocks_per_batch = self.per_batch // TM
        self.n_blocks = n_batch * self.blocks_per_batch
        self.t_all = n_batch * self.per_batch

    def mod_index(self, j):
        jb = j % self.blocks_per_batch
        return jnp.where(jb == 0, self.n_batch, j // self.blocks_per_batch)

    def mod_spec(self, d):
        return pl.BlockSpec((1, 6, d), lambda j: (self.mod_index(j), 0, 0))

    def row_spec(self, width):
        return pl.BlockSpec((TM, width), lambda j: (j, 0))

    def pos_spec(self, width):
        return pl.BlockSpec((TM, width), lambda j: (j % self.blocks_per_batch, 0))


def _full(shape):
    return pl.BlockSpec(shape, lambda *_: (0,) * len(shape))


def _qkv_kernel(x_ref, mod_ref, g_ref, w_ref, cos_ref, sina_ref, sinb_ref, qg_ref, kg_ref, seg_ref,
                q_ref, k_ref, v_ref, *, n_q, n_kv, qk_norm):
    h = _modulate(x_ref[...], g_ref[...], mod_ref[0, 0:1], mod_ref[0, 1:2])
    y = _dot(h.astype(BF16), w_ref[...])
    cos, sina, sinb = cos_ref[...], sina_ref[...], sinb_ref[...]
    seg = seg_ref[...]
    lane = lax.broadcasted_iota(jnp.int32, (TM, LANES), 1)
    low = lane < HEAD
    ones_col = jnp.where(lane == HEAD, 1.0, 0.0)

    def head_pair(t, gain):
        if qk_norm:
            t = t * lax.rsqrt(_seg_apply(t * t, seg) + NORM_EPS) * gain
        return t * cos + pltpu.roll(t, LANES - 16, 1) * sina + pltpu.roll(t, 16, 1) * sinb

    def both_heads(t):
        return t, pltpu.roll(t, HEAD, 1)

    for i in range(n_q // LANES):
        t = head_pair(y[:, i * LANES:(i + 1) * LANES], qg_ref[...]) * Q_SCALE
        for u, th in enumerate(both_heads(t)):
            q_ref[:, (2 * i + u) * LANES:(2 * i + u + 1) * LANES] = th.astype(BF16)
    for i in range(n_kv // LANES):
        t = head_pair(y[:, n_q + i * LANES:n_q + (i + 1) * LANES], kg_ref[...])
        for u, th in enumerate(both_heads(t)):
            k_ref[:, (2 * i + u) * LANES:(2 * i + u + 1) * LANES] = jnp.where(low, th, 0.0).astype(BF16)
        t = y[:, n_q + n_kv + i * LANES:n_q + n_kv + (i + 1) * LANES]
        for u, th in enumerate(both_heads(t)):
            v_ref[:, (2 * i + u) * LANES:(2 * i + u + 1) * LANES] = jnp.where(low, th, ones_col).astype(BF16)


def _qkv_project(lay, x, mod, g, w, rope, q_gain, k_gain, seg_mean, *, n_kv_heads, qk_norm):
    d = x.shape[1]
    n_q, n_kv = d, n_kv_heads * HEAD
    cos, sina, sinb = rope
    pair = lambda v: jnp.tile(v.reshape(1, HEAD), (1, LANES // HEAD))
    return pl.pallas_call(
        functools.partial(_qkv_kernel, n_q=n_q, n_kv=n_kv, qk_norm=qk_norm),
        grid=(lay.n_blocks,),
        in_specs=[lay.row_spec(d), lay.mod_spec(d), _full((1, d)), _full(w.shape),
                  lay.pos_spec(LANES), lay.pos_spec(LANES), lay.pos_spec(LANES),
                  _full((1, LANES)), _full((1, LANES)), _full((LANES, LANES))],
        out_specs=[lay.row_spec(2 * n_q), lay.row_spec(2 * n_kv), lay.row_spec(2 * n_kv)],
        out_shape=[jax.ShapeDtypeStruct((lay.t_all, 2 * n_q), BF16),
                   jax.ShapeDtypeStruct((lay.t_all, 2 * n_kv), BF16),
                   jax.ShapeDtypeStruct((lay.t_all, 2 * n_kv), BF16)],
        compiler_params=_params(1),
        name="qkv_project",
    )(x, mod, g.reshape(1, d), w.astype(BF16), cos, sina, sinb, pair(q_gain), pair(k_gain), seg_mean)


def _rope_tables(lay):
    rows = lay.seq // 64
    grid_w = 64
    row = jnp.repeat(jnp.arange(rows, dtype=F32), grid_w)
    col = jnp.tile(jnp.arange(grid_w, dtype=F32), rows)
    axis_dim = HEAD // 2
    inv_freq = ROPE_THETA ** (-jnp.arange(0, axis_dim, 2, dtype=F32) / axis_dim)
    a_row = row[:, None] * inv_freq
    a_col = col[:, None] * inv_freq
    ang = jnp.concatenate([a_row, a_row, a_col, a_col], axis=-1)
    cos = jnp.concatenate([jnp.ones((lay.n_ctx, HEAD), F32), jnp.cos(ang)], axis=0)
    sin = jnp.concatenate([jnp.zeros((lay.n_ctx, HEAD), F32), jnp.sin(ang)], axis=0)
    first = (jnp.arange(HEAD) % 32) < 16
    sina = jnp.where(first, -sin, 0.0)
    sinb = jnp.where(first, 0.0, sin)
    wide = lambda t: jnp.tile(t, (1, LANES // HEAD))
    return wide(cos), wide(sina), wide(sinb)


def _attn_kernel(sink_ref, q_ref, k_ref, v_ref, o_ref, *, n_kv, grp, window, n_ctx, seq):
    tq = ATT_TQ
    j = pl.program_id(1)
    rows = grp * tq
    ctx_blocks = n_ctx // tq
    gs = range(n_kv)
    tile = lambda ref, r0, g: ref[r0, g * LANES:(g + 1) * LANES]
    qg = [jnp.concatenate([tile(q_ref, slice(None), g * grp + u) for u in range(grp)], axis=0) for g in gs]

    def step(carry, ks, vs, keep=None):
        s = [_dot_nt(qg[g], ks[g]) for g in gs]
        if keep is not None:
            s = [jnp.where(keep, s[g], NEG_INF) for g in gs]
        m_new = [jnp.maximum(carry[g][0], jnp.max(s[g], axis=1, keepdims=True)) for g in gs]
        p = [jnp.exp(s[g] - m_new[g]).astype(BF16) for g in gs]
        alpha = [jnp.exp(carry[g][0] - m_new[g]) for g in gs]
        pv = [_dot(p[g], vs[g]) for g in gs]
        return tuple((m_new[g], alpha[g] * carry[g][1] + pv[g]) for g in gs)

    if window:
        lane = lax.broadcasted_iota(jnp.int32, (rows, LANES), 1)
        acc0 = jnp.where(lane == HEAD, 1.0, 0.0)
        carry = tuple((jnp.concatenate([jnp.full((tq, 1), sink_ref[g * grp + u], F32) for u in range(grp)],
                                       axis=0), acc0) for g in gs)
        tk = WINDOW
        rr = lax.broadcasted_iota(jnp.int32, (rows, tk), 0) & (tq - 1)
        cc = lax.broadcasted_iota(jnp.int32, (rows, tk), 1)
        j_lat = j - ctx_blocks
        n_lat = seq // tq
        never = 4 * tk
        for cb in range(n_ctx // tk):
            r0 = slice(cb * tk, (cb + 1) * tk)
            carry = step(carry, [tile(k_ref, r0, g) for g in gs], [tile(v_ref, r0, g) for g in gs])
        for off in (-1, 0, 1):
            blk = j_lat + off
            valid = jnp.logical_and(j_lat >= 0, jnp.logical_and(blk >= 0, blk < n_lat))
            r0 = pl.ds(pl.multiple_of(n_ctx + jnp.clip(blk, 0, n_lat - 1) * tk, tk), tk)
            bar = jnp.where(valid, 0, never)
            keep = {-1: cc >= rr + bar, 0: cc >= bar, 1: cc + bar <= rr}[off]
            carry = step(carry, [tile(k_ref, r0, g) for g in gs], [tile(v_ref, r0, g) for g in gs], keep)
    else:
        tk = min(ATT_TK, seq)
        assert seq % tk == 0
        carry = tuple((jnp.full((rows, 1), NEG_INF, F32), jnp.zeros((rows, LANES), F32)) for g in gs)
        ctx_rows = slice(0, n_ctx)
        carry = step(carry, [tile(k_ref, ctx_rows, g) for g in gs], [tile(v_ref, ctx_rows, g) for g in gs])
        n_tiles = jnp.where(j < ctx_blocks, 0, seq // tk)

        def body(t, c):
            r0 = pl.ds(pl.multiple_of(n_ctx + t * tk, math.gcd(n_ctx, tk)), tk)
            return step(c, [tile(k_ref, r0, g) for g in gs], [tile(v_ref, r0, g) for g in gs])

        carry = lax.fori_loop(0, n_tiles, body, carry)
    for g in gs:
        acc = carry[g][1]
        out = acc[:, :HEAD] / acc[:, HEAD:HEAD + 1]
        for u in range(grp):
            hd = g * grp + u
            o_ref[:, hd * HEAD:(hd + 1) * HEAD] = out[u * tq:(u + 1) * tq].astype(BF16)


def _attention(lay, q, k, v, sink, *, n_kv_heads, window):
    d = q.shape[1] // 2
    n_kv = n_kv_heads * LANES
    per_q = lay.per_batch // ATT_TQ
    kv_spec = pl.BlockSpec((lay.per_batch, n_kv), lambda b, j, *_: (b, 0), pipeline_mode=pl.Buffered(1))
    q_spec = pl.BlockSpec((ATT_TQ, 2 * d), lambda b, j, *_: (b * per_q + j, 0))
    o_spec = pl.BlockSpec((ATT_TQ, d), lambda b, j, *_: (b * per_q + j, 0))
    return pl.pallas_call(
        functools.partial(_attn_kernel, n_kv=n_kv_heads, grp=d // HEAD // n_kv_heads, window=window,
                          n_ctx=lay.n_ctx, seq=lay.seq),
        grid_spec=pltpu.PrefetchScalarGridSpec(
            num_scalar_prefetch=1, grid=(lay.n_batch, per_q),
            in_specs=[q_spec, kv_spec, kv_spec], out_specs=o_spec),
        out_shape=jax.ShapeDtypeStruct((lay.t_all, d), BF16),
        compiler_params=_params(2),
        name="window_attention" if window else "dense_attention",
    )(sink, q, k, v)


def _route(logits):
    lane = lax.broadcasted_iota(jnp.int32, logits.shape, 1)
    far = 4 * LANES
    is_g = lane < N_GROUPS
    gmax = jnp.max(jnp.where(is_g, logits, NEG_INF), axis=1, keepdims=True)
    gidx = jnp.min(jnp.where(jnp.logical_and(is_g, logits == gmax), lane, far), axis=1, keepdims=True)
    gsum = jnp.sum(jnp.where(is_g, jnp.exp(logits - gmax), 0.0), axis=1, keepdims=True)
    g_w = 1.0 / gsum
    is_e = jnp.logical_and(lane >= N_GROUPS, ((lane - N_GROUPS) >> 3) == gidx)
    emax = jnp.max(jnp.where(is_e, logits, NEG_INF), axis=1, keepdims=True)
    ee = jnp.where(is_e, jnp.exp(logits - emax), 0.0)
    p = ee / jnp.sum(ee, axis=1, keepdims=True)
    p_in = jnp.where(is_e, p, -1.0)
    p1 = jnp.max(p_in, axis=1, keepdims=True)
    i1 = jnp.min(jnp.where(p_in == p1, lane, far), axis=1, keepdims=True)
    p_rest = jnp.where(lane == i1, -1.0, p_in)
    p2 = jnp.max(p_rest, axis=1, keepdims=True)
    i2 = jnp.min(jnp.where(p_rest == p2, lane, far), axis=1, keepdims=True)
    tot = p1 + p2
    w1 = g_w * p1 / tot
    w2 = g_w * p2 / tot
    e1 = (i1 - N_GROUPS).astype(F32)
    e2 = (i2 - N_GROUPS).astype(F32)
    picks = (jnp.where(lane == i1, 1.0, 0.0), jnp.where(lane == i2, 1.0, 0.0))
    return (e1, e2, w1, w2), picks


def _post_kernel(o_ref, wo_ref, x_ref, mod_ref, g2_ref, wr_hi_ref, wr_lo_ref, br_ref,
                 xn_ref, h2_ref, route_ref, count_ref, seen_ref):
    @pl.when(pl.program_id(0) == 0)
    def _():
        seen_ref[...] = jnp.zeros_like(seen_ref)

    y = _dot(o_ref[...], wo_ref[...])
    xn = x_ref[...] + mod_ref[0, 2:3] * y
    xn_ref[...] = xn
    h2 = _modulate(xn, g2_ref[...], mod_ref[0, 3:4], mod_ref[0, 4:5])
    h2_ref[...] = h2
    h_hi = _bf(h2)
    h_lo = _bf(h2 - h_hi.astype(F32))
    logits = _dot(h_hi, wr_hi_ref[...]) + (_dot(h_hi, wr_lo_ref[...]) + _dot(h_lo, wr_hi_ref[...]))
    cols, (pick1, pick2) = _route(logits + br_ref[...])
    both = pick1 + pick2
    row = lax.broadcasted_iota(jnp.int32, (TM, TM), 0)
    col = lax.broadcasted_iota(jnp.int32, (TM, TM), 1)
    earlier = jnp.where(col < row, 1.0, 0.0).astype(BF16)
    before = _dot(earlier, both.astype(BF16)) + seen_ref[0:1]
    ranks = (jnp.sum(before * pick1, axis=1, keepdims=True), jnp.sum(before * pick2, axis=1, keepdims=True))
    seen = seen_ref[0:1] + jnp.sum(both, axis=0, keepdims=True)
    seen_ref[...] = jnp.broadcast_to(seen, seen_ref.shape)
    count_ref[...] = jnp.broadcast_to(seen, count_ref.shape)
    lane = lax.broadcasted_iota(jnp.int32, (TM, LANES), 1)
    out = jnp.zeros((TM, LANES), F32)
    for i, c in enumerate(cols + ranks):
        out = jnp.where(lane == i, c, out)
    route_ref[...] = out


def _post_mixer(lay, o, w_o, x, mod, g2, w_route, b_route):
    d = x.shape[1]
    w_hi = w_route.astype(BF16)
    return pl.pallas_call(
        _post_kernel,
        grid=(lay.n_blocks,),
        in_specs=[lay.row_spec(d), _full((d, d)), lay.row_spec(d), lay.mod_spec(d), _full((1, d)),
                  _full((d, LANES)), _full((d, LANES)), _full((1, LANES))],
        out_specs=[lay.row_spec(d), lay.row_spec(d), lay.row_spec(LANES), _full((8, LANES))],
        out_shape=[jax.ShapeDtypeStruct((lay.t_all, d), F32), jax.ShapeDtypeStruct((lay.t_all, d), F32),
                   jax.ShapeDtypeStruct((lay.t_all, LANES), F32), jax.ShapeDtypeStruct((8, LANES), F32)],
        scratch_shapes=[pltpu.VMEM((8, LANES), F32)],
        compiler_params=_params(1),
        name="post_mixer",
    )(o, w_o.astype(BF16), x, mod, g2.reshape(1, d), w_hi, (w_route - w_hi.astype(F32)).astype(BF16), b_route)


def _dispatch_kernel(dest_ref, dest_prev_ref, h_hbm, xs_in, xs_hbm, sem):
    del xs_in
    j = pl.program_id(0)
    last = pl.num_programs(0) - 1
    slot = j % 2

    def row_copy(d_ref, block, r, k, s):
        return pltpu.make_async_copy(h_hbm.at[pl.ds(block * TM + r, 1)], xs_hbm.at[pl.ds(d_ref[0, k, r], 1)], sem.at[s])

    def start(r, carry):
        row_copy(dest_ref, j, r, 0, slot).start()
        row_copy(dest_ref, j, r, 1, slot).start()
        return carry

    def wait_prev(r, carry):
        row_copy(dest_prev_ref, j - 1, r, 0, 1 - slot).wait()
        row_copy(dest_prev_ref, j - 1, r, 1, 1 - slot).wait()
        return carry

    def wait_own(r, carry):
        row_copy(dest_ref, j, r, 0, slot).wait()
        row_copy(dest_ref, j, r, 1, slot).wait()
        return carry

    lax.fori_loop(0, TM, start, 0, unroll=8)

    @pl.when(j > 0)
    def _():
        lax.fori_loop(0, TM, wait_prev, 0, unroll=8)

    @pl.when(j == last)
    def _():
        lax.fori_loop(0, TM, wait_own, 0, unroll=8)


def _dispatch(lay, h2, dest_blocks, n_rows):
    d = h2.shape[1]
    dest_spec = lambda idx: pl.BlockSpec((1, 2, TM), lambda j: (idx(j), 0, 0), memory_space=pltpu.SMEM)
    return pl.pallas_call(
        _dispatch_kernel,
        grid=(lay.n_blocks,),
        in_specs=[dest_spec(lambda j: j), dest_spec(lambda j: jnp.maximum(j - 1, 0)),
                  pl.BlockSpec(memory_space=pl.ANY), pl.BlockSpec(memory_space=pl.ANY)],
        out_specs=pl.BlockSpec(memory_space=pl.ANY),
        out_shape=jax.ShapeDtypeStruct((n_rows, d), F32),
        scratch_shapes=[pltpu.SemaphoreType.DMA((2,))],
        input_output_aliases={3: 0},
        compiler_params=_params(1),
        name="moe_dispatch",
    )(dest_blocks, dest_blocks, h2, jnp.zeros((n_rows, d), F32))


def _expert_kernel(blk_ref, used_ref, x_ref, w1_ref, w3_ref, w2_ref, out_ref):
    del blk_ref
    i = pl.program_id(0)

    @pl.when(i < used_ref[0])
    def _():
        xb = x_ref[...].astype(BF16)
        a = _dot(xb, w1_ref[0].astype(BF16))
        hid = (a * _sigmoid(a)) * _dot(xb, w3_ref[0].astype(BF16))
        out_ref[...] = _dot(hid.astype(BF16), w2_ref[0].astype(BF16))

    @pl.when(i >= used_ref[0])
    def _():
        out_ref[...] = jnp.zeros_like(out_ref)


def _expert_blocks(xs, blk_expert, n_used, w1, w3, w2, layer):
    n_rows, d = xs.shape
    n_blocks = n_rows // MOE_ROWS
    de = w1.shape[2]
    base = layer * N_EXPERTS
    return pl.pallas_call(
        _expert_kernel,
        grid_spec=pltpu.PrefetchScalarGridSpec(
            num_scalar_prefetch=2, grid=(n_blocks,),
            in_specs=[pl.BlockSpec((MOE_ROWS, d), lambda i, blk, used: (i, 0)),
                      pl.BlockSpec((1, d, de), lambda i, blk, used: (base + blk[i], 0, 0)),
                      pl.BlockSpec((1, d, de), lambda i, blk, used: (base + blk[i], 0, 0)),
                      pl.BlockSpec((1, de, d), lambda i, blk, used: (base + blk[i], 0, 0))],
            out_specs=pl.BlockSpec((MOE_ROWS, d), lambda i, blk, used: (i, 0))),
        out_shape=jax.ShapeDtypeStruct((n_rows, d), F32),
        compiler_params=_params(1),
        name="expert_blocks",
    )(blk_expert, n_used, xs, w1, w3, w2)


def _combine_kernel(pos_ref, pos_next_ref, ys_hbm, x_ref, mod_ref, route_ref, o_ref, buf, sem):
    j = pl.program_id(0)
    last = pl.num_programs(0) - 1
    slot = j % 2

    def row_copy(p_ref, r, k, s):
        return pltpu.make_async_copy(ys_hbm.at[pl.ds(p_ref[0, k, r], 1)], buf.at[s, k, pl.ds(r, 1)], sem.at[s])

    def issue(p_ref, s):
        def start(r, carry):
            row_copy(p_ref, r, 0, s).start()
            row_copy(p_ref, r, 1, s).start()
            return carry
        lax.fori_loop(0, TM, start, 0, unroll=8)

    @pl.when(j == 0)
    def _():
        issue(pos_ref, 0)

    @pl.when(j < last)
    def _():
        issue(pos_next_ref, 1 - slot)

    def wait(r, carry):
        row_copy(pos_ref, r, 0, slot).wait()
        row_copy(pos_ref, r, 1, slot).wait()
        return carry

    lax.fori_loop(0, TM, wait, 0, unroll=8)
    route = route_ref[...]
    f = buf[slot, 0] * route[:, 2:3] + buf[slot, 1] * route[:, 3:4]
    o_ref[...] = x_ref[...] + mod_ref[0, 5:6] * f


def _combine(lay, ys, pos, x, mod, route):
    d = x.shape[1]
    pos_spec = lambda idx: pl.BlockSpec((1, 2, TM), lambda j: (idx(j), 0, 0), memory_space=pltpu.SMEM)
    return pl.pallas_call(
        _combine_kernel,
        grid=(lay.n_blocks,),
        in_specs=[pos_spec(lambda j: j), pos_spec(lambda j: jnp.minimum(j + 1, lay.n_blocks - 1)),
                  pl.BlockSpec(memory_space=pl.ANY), lay.row_spec(d), lay.mod_spec(d), lay.row_spec(LANES)],
        out_specs=lay.row_spec(d),
        out_shape=jax.ShapeDtypeStruct((lay.t_all, d), F32),
        scratch_shapes=[pltpu.VMEM((2, 2, TM, d), F32), pltpu.SemaphoreType.DMA((2,))],
        compiler_params=_params(1),
        name="moe_combine",
    )(pos, pos, ys, x, mod, route)


def _dispatch_plan(route, counts):
    n_tok = route.shape[0]
    n_blocks = -(-(n_tok * 2) // MOE_ROWS) + N_EXPERTS
    counts = counts[0, N_GROUPS:N_GROUPS + N_EXPERTS].astype(jnp.int32)
    padded = (counts + MOE_ROWS - 1) // MOE_ROWS * MOE_ROWS
    p_end = jnp.cumsum(padded)
    first_slot = p_end - padded
    expert = route[:, 0:2].astype(jnp.int32)
    onehot = expert[:, :, None] == jnp.arange(N_EXPERTS, dtype=jnp.int32)
    dest = jnp.sum(jnp.where(onehot, first_slot, 0), axis=-1) + route[:, 4:6].astype(jnp.int32)
    block_start = jnp.arange(n_blocks, dtype=jnp.int32) * MOE_ROWS
    blk_expert = jnp.minimum(jnp.sum(p_end[None, :] <= block_start[:, None], axis=1), N_EXPERTS - 1).astype(jnp.int32)
    n_used = (p_end[-1:] // MOE_ROWS).astype(jnp.int32)
    return dest, blk_expert, n_used, n_blocks


def _moe(lay, xn, h2, route, counts, mod, w1, w3, w2, layer):
    dest, blk_expert, n_used, n_blocks = _dispatch_plan(route, counts)
    dest_blocks = dest.reshape(lay.n_blocks, TM, 2).transpose(0, 2, 1)
    xs = _dispatch(lay, h2, dest_blocks, n_blocks * MOE_ROWS)
    ys = _expert_blocks(xs, blk_expert, n_used, w1, w3, w2, layer)
    return _combine(lay, ys, dest_blocks, xn, mod, route)


def _rwkv_feat_kernel(x_ref, xp_ref, xq_ref, mod_ref, g_ref, mu_ref, wr_ref, wk_ref, wv_ref, w0_ref, w1_ref, w2_ref,
                      a0_ref, a1_ref, a2_ref, g1_ref, g2_ref, kk_ref, ka_ref, seg_ref,
                      r_ref, v_ref, kn_ref, gate_ref, lw_ref, kd_ref, b_ref, *, blocks_per_batch):
    jb = pl.program_id(0) % blocks_per_batch
    shift, scale, g = mod_ref[0, 0:1], mod_ref[0, 1:2], g_ref[...]
    h = _modulate(x_ref[...], g, shift, scale)
    has_prev = jnp.where(jb > 1, 1.0, 0.0)
    has_next = jnp.where(jnp.logical_and(jb > 0, jb < blocks_per_batch - 1), 1.0, 0.0)
    h_before = _modulate(xp_ref[...], g, shift, scale)[7:8] * has_prev
    h_after = _modulate(xq_ref[...], g, shift, scale)[0:1] * has_next
    rows = lax.broadcasted_iota(jnp.int32, h.shape, 0)
    prev = jnp.where(rows == 0, h_before, pltpu.roll(h, 1, 0))
    nxt = jnp.where(rows == TM - 1, h_after, pltpu.roll(h, TM - 1, 0))
    xx = 0.5 * (prev + nxt) - h
    mix = lambda i: (h + xx * mu_ref[i:i + 1]).astype(BF16)
    xr, xw, xk, xv, xa, xg = [mix(i) for i in range(6)]
    r = _dot(xr, wr_ref[...])
    k = _dot(xk, wk_ref[...])
    r_ref[...] = r
    v_ref[...] = _dot(xv, wv_ref[...])
    gate_ref[...] = _dot(_sigmoid(_dot(xg, g1_ref[...])).astype(BF16), g2_ref[...])
    kk = k * kk_ref[...]
    norm = jnp.sqrt(_seg_apply(kk * kk, seg_ref[...]))
    kk = kk / jnp.maximum(norm, 1e-12)
    kn_ref[...] = kk
    for n in range(2):
        w_log = w0_ref[n:n + 1] + _dot(jnp.tanh(_dot(xw, w1_ref[n])).astype(BF16), w2_ref[n])
        z = -w_log
        softplus = jnp.maximum(z, 0.0) + jnp.log(1.0 + jnp.exp(-jnp.abs(z)))
        lw_ref[n] = -jnp.exp(-softplus - 0.5)
        iclr = _sigmoid(a0_ref[n:n + 1] + _dot(_dot(xa, a1_ref[n]).astype(BF16), a2_ref[n]))
        kd_ref[n] = k * (1.0 + (iclr - 1.0) * ka_ref[...])
        b_ref[n] = kk * iclr


def _rwkv_features(lay, x, mod, g, p, seg_sum):
    (mu, w_r, w_k, w_v, w0, w1, w2, a0, a1, a2, g1, g2, k_k, k_a) = p
    d = x.shape[1]
    row = lay.row_spec(d)
    sub = TM // 8
    last = lay.t_all // 8 - 1
    before = pl.BlockSpec((8, d), lambda j: (jnp.maximum(j * sub - 1, 0), 0))
    after = pl.BlockSpec((8, d), lambda j: (jnp.minimum((j + 1) * sub, last), 0))
    both = pl.BlockSpec((2, TM, d), lambda j: (0, j, 0))
    bf = lambda t: t.astype(BF16)
    args = (x, x, x, mod, g.reshape(1, d), mu, bf(w_r), bf(w_k), bf(w_v), w0, bf(w1), bf(w2), a0, bf(a1), bf(a2),
            bf(g1), bf(g2), k_k.reshape(1, d), k_a.reshape(1, d), seg_sum)
    in_specs = [row, before, after, lay.mod_spec(d)] + [_full(a.shape) for a in args[4:]]
    shape = jax.ShapeDtypeStruct((lay.t_all, d), F32)
    shape2 = jax.ShapeDtypeStruct((2, lay.t_all, d), F32)
    return pl.pallas_call(
        functools.partial(_rwkv_feat_kernel, blocks_per_batch=lay.blocks_per_batch),
        grid=(lay.n_blocks,),
        in_specs=in_specs,
        out_specs=[row, row, row, row, both, both, both],
        out_shape=[shape, shape, shape, shape, shape2, shape2, shape2],
        compiler_params=_params(1),
        name="rwkv_features",
    )(*args)


def _scan_operands(sign, r_ref, v_ref, kk_ref, lw_ref, kd_ref, b_ref, n_heads):
    c = SCAN_CHUNK
    row = lax.broadcasted_iota(jnp.int32, (c, c), 0)
    col = lax.broadcasted_iota(jnp.int32, (c, c), 1)
    incl = (row - col) * sign >= 0
    lw = lw_ref[0]
    tri = jnp.where(incl, 1.0, 0.0).astype(BF16)
    lw_hi = _bf(lw)
    lw_lo = _bf(lw - lw_hi.astype(F32))
    lg = _dot(jnp.concatenate([tri, tri], axis=1), jnp.concatenate([lw_hi, lw_lo], axis=0))
    lg_tot = jnp.sum(lw, axis=0, keepdims=True)
    g_inv = jnp.exp(-lg)
    g_rem = jnp.exp(lg_tot - lg)
    g_tot = jnp.exp(lg_tot)
    v_all = v_ref[...]
    even = (lax.broadcasted_iota(jnp.int32, lw.shape, 1) & HEAD) == 0
    halves = lambda t: (_bf(jnp.where(even, t, 0.0)), _bf(jnp.where(even, 0.0, t)))
    ar_all = _bf(jnp.concatenate([-kk_ref[...] * jnp.exp(lg - lw), r_ref[...] * jnp.exp(lg)], axis=0))
    b_t, k_t = halves(b_ref[0] * g_inv), halves(kd_ref[0] * g_inv)
    b_h, k_h = halves(b_ref[0] * g_rem), halves(kd_ref[0] * g_rem)

    lag2 = (lax.broadcasted_iota(jnp.int32, (c, LANES), 0)
            - (lax.broadcasted_iota(jnp.int32, (c, LANES), 1) & (HEAD - 1))) * sign
    hs = range(n_heads)
    pair = [slice(h // 2 * LANES, (h // 2 + 1) * LANES) for h in hs]
    return dict(
        strict2=[lag2 > 0] * n_heads,
        incl2=[lag2 >= 0] * n_heads,
        ar=[ar_all[:, pair[h]] for h in hs],
        bk=[jnp.concatenate([b_t[h % 2][:, pair[h]], k_t[h % 2][:, pair[h]], b_t[h % 2][:, pair[h]],
                             b_t[h % 2][:, pair[h]]], axis=0) for h in hs],
        bkh=[jnp.concatenate([b_h[h % 2][:, pair[h]], k_h[h % 2][:, pair[h]]], axis=0) for h in hs],
        v_h=[v_all[:, h * HEAD:(h + 1) * HEAD] for h in hs],
        g_tot=[g_tot[:, pair[h]] for h in hs])


def _scan_kernel(rf_ref, vf_ref, kkf_ref, lwf_ref, kdf_ref, bf_ref, rb_ref, vb_ref, kkb_ref, lwb_ref, kdb_ref, bb_ref,
                 yf_ref, yb_ref, s_ref, *, n_heads):
    c = SCAN_CHUNK

    @pl.when(pl.program_id(1) == 0)
    def _():
        s_ref[...] = jnp.zeros_like(s_ref)

    fwd = _scan_operands(1, rf_ref, vf_ref, kkf_ref, lwf_ref, kdf_ref, bf_ref, n_heads)
    bwd = _scan_operands(-1, rb_ref, vb_ref, kkb_ref, lwb_ref, kdb_ref, bb_ref, n_heads)
    op = {key: fwd[key] + bwd[key] for key in fwd}
    hs = range(2 * n_heads)
    zeros_v = jnp.zeros((c, HEAD), BF16)
    zeros_l = jnp.zeros((c, LANES), BF16)
    s0 = [s_ref[h] for h in hs]
    p = [_dot_nt(op["ar"][h], op["bk"][h]) for h in hs]
    ars = [_dot_nt(op["ar"][h], _bf(s0[h])) for h in hs]
    x = [ars[h][:c] + _dot(_bf(jnp.where(op["strict2"][h], p[h][:c, :LANES], 0.0)),
                           jnp.concatenate([zeros_v, _bf(op["v_h"][h])], axis=0)) for h in hs]
    pw = [_bf(jnp.where(op["strict2"][h], p[h][:c, LANES:], 0.0)) for h in hs]
    x = [x[h] + _apply_split(pw[h], x[h]) for h in hs]
    for _ in range(5):
        pw = [_bf(_dot(pw[h], jnp.concatenate([pw[h], zeros_l], axis=0))) for h in hs]
        x = [x[h] + _apply_split(pw[h], x[h]) for h in hs]
    uv = [_bf(jnp.concatenate([x[h], op["v_h"][h]], axis=0)) for h in hs]
    y = [ars[h][c:] + _dot(_bf(jnp.where(op["incl2"][h], p[h][c:, :LANES], 0.0)), uv[h]) for h in hs]
    yf_ref[...] = jnp.concatenate(y[:n_heads], axis=1)
    yb_ref[...] = jnp.concatenate(y[n_heads:], axis=1)
    states = [s0[h] * op["g_tot"][h] + _dot_tn(uv[h], op["bkh"][h]) for h in hs]
    for h in hs:
        s_ref[h] = states[h]


def _bf(t):
    return t.astype(BF16)


def _apply_split(l_dup, x):
    x_hi = _bf(x)
    x_lo = _bf(x - x_hi.astype(F32))
    return _dot(l_dup, jnp.concatenate([x_hi, x_lo], axis=0))


def _rwkv_scan(r, v, kk, lw, kd, b, *, n_batch, n_ctx):
    t_all, d = r.shape
    c = SCAN_CHUNK
    per_batch = t_all // n_batch // c
    ctx_chunks = n_ctx // c
    n_heads = d // HEAD

    fwd_chunk = lambda bi, i: bi * per_batch + i
    bwd_chunk = lambda bi, i: bi * per_batch + jnp.where(i < ctx_chunks, ctx_chunks - 1 - i,
                                                         per_batch - 1 - (i - ctx_chunks))
    specs = []
    for dr, chunk in ((0, fwd_chunk), (1, bwd_chunk)):
        shared = pl.BlockSpec((c, d), lambda bi, i, chunk=chunk: (chunk(bi, i), 0))
        per_dir = pl.BlockSpec((1, c, d), lambda bi, i, chunk=chunk, dr=dr: (dr, chunk(bi, i), 0))
        specs.append((shared, per_dir))
    in_specs = [sp for shared, per_dir in specs for sp in (shared, shared, shared, per_dir, per_dir, per_dir)]
    out = jax.ShapeDtypeStruct((t_all, d), F32)
    return pl.pallas_call(
        functools.partial(_scan_kernel, n_heads=n_heads),
        grid=(n_batch, per_batch),
        in_specs=in_specs,
        out_specs=[specs[0][0], specs[1][0]],
        out_shape=[out, out],
        scratch_shapes=[pltpu.VMEM((2 * n_heads, HEAD, LANES), F32)],
        compiler_params=_params(2),
        name="rwkv_scan",
    )(r, v, kk, lw, kd, b, r, v, kk, lw, kd, b)


def _rwkv_out_kernel(yf_ref, yb_ref, r_ref, v_ref, gate_ref, kd_ref, rk_ref, lnw_ref, lnb_ref, seg_ref, o_ref):
    seg = seg_ref[...]
    y = yf_ref[...] + yb_ref[...]
    mean = _seg_apply(y, seg)
    cen = y - mean
    var = _seg_apply(cen * cen, seg)
    yn = cen * lax.rsqrt(var + LN_X_EPS) * lnw_ref[...] + lnb_ref[...]
    bonus = _seg_apply(r_ref[...] * (kd_ref[0] + kd_ref[1]) * rk_ref[...], seg) * float(HEAD) * v_ref[...]
    o_ref[...] = ((yn + bonus) * gate_ref[...]).astype(BF16)


def _rwkv_out(lay, y_f, y_b, r, v, gate, kd, r_k, ln_w, ln_b, seg_mean):
    d = r.shape[1]
    row = lay.row_spec(d)
    both = pl.BlockSpec((2, TM, d), lambda j: (0, j, 0))
    return pl.pallas_call(
        _rwkv_out_kernel,
        grid=(lay.n_blocks,),
        in_specs=[row, row, row, row, row, both, _full((1, d)), _full((1, d)), _full((1, d)), _full((LANES, LANES))],
        out_specs=row,
        out_shape=jax.ShapeDtypeStruct((lay.t_all, d), BF16),
        compiler_params=_params(1),
        name="rwkv_out",
    )(y_f, y_b, r, v, gate, kd, r_k.reshape(1, d), ln_w.reshape(1, d), ln_b.reshape(1, d), seg_mean)


def _final_kernel(x_ref, g_ref, o_ref):
    x = x_ref[...]
    o_ref[0] = x * lax.rsqrt(jnp.mean(x * x, axis=-1, keepdims=True) + NORM_EPS) * g_ref[...]


def _final_norm(lay, x, g):
    d = x.shape[1]
    lat_blocks = lay.seq // TM
    return pl.pallas_call(
        _final_kernel,
        grid=(lay.n_batch, lat_blocks),
        in_specs=[pl.BlockSpec((TM, d), lambda b, j: (b * lay.blocks_per_batch + 1 + j, 0)), _full((1, d))],
        out_specs=pl.BlockSpec((1, TM, d), lambda b, j: (b, j, 0)),
        out_shape=jax.ShapeDtypeStruct((lay.n_batch, lay.seq, d), F32),
        compiler_params=_params(2),
        name="final_norm",
    )(x, g.reshape(1, d))


def kernel(x, c, ctx, c_ctx, mod_w, mod_b, norm1_g, norm2_g, a_w_qkv, a_w_o, a_q_gain, a_k_gain, b_mu, b_w_r, b_w_k, b_w_v, b_w_o, b_decay_w0, b_decay_w1, b_decay_w2, b_iclr_a0, b_iclr_a1, b_iclr_a2, b_gate_g1, b_gate_g2, b_k_k, b_k_a, b_r_k, b_ln_w, b_ln_b, c_w_qkv, c_w_o, c_sink, moe_w_group, moe_b_group, moe_w_expert, moe_b_expert, moe_w1, moe_w3, moe_w2, final_g):
    n_batch, seq, d = x.shape
    n_ctx = ctx.shape[1]
    depth = mod_w.shape[0]
    lay = _Layout(n_batch, n_ctx, seq)

    cond = jnp.zeros((8, d), F32).at[:n_batch].set(c).at[n_batch].set(c_ctx)
    mods = _mod_vectors(cond, mod_w, mod_b)[:, :n_batch + 1].reshape(depth, n_batch + 1, 6, d)
    xs = jnp.concatenate([ctx, x], axis=1).reshape(lay.t_all, d)

    blockdiag = (jnp.arange(LANES)[:, None] // HEAD == jnp.arange(LANES)[None, :] // HEAD).astype(F32)
    seg_sum, seg_mean = blockdiag, blockdiag / HEAD
    rope = _rope_tables(lay)
    no_gain = jnp.ones((HEAD,), F32)
    stacked = lambda w: w.reshape((-1,) + w.shape[2:])
    w1_all, w3_all, w2_all = stacked(moe_w1), stacked(moe_w3), stacked(moe_w2)
    no_sink = jnp.zeros((d // HEAD,), F32)

    for l in range(depth):
        kind, j = l % 3, l // 3
        mod = mods[l]
        if kind == 0:
            q, k, v = _qkv_project(lay, xs, mod, norm1_g[l], a_w_qkv[j], rope, a_q_gain[j], a_k_gain[j], seg_mean,
                                   n_kv_heads=4, qk_norm=True)
            o = _attention(lay, q, k, v, no_sink, n_kv_heads=4, window=False)
            w_o = a_w_o[j]
        elif kind == 1:
            p = (b_mu[j], b_w_r[j], b_w_k[j], b_w_v[j], b_decay_w0[j], b_decay_w1[j], b_decay_w2[j],
                 b_iclr_a0[j], b_iclr_a1[j], b_iclr_a2[j], b_gate_g1[j], b_gate_g2[j], b_k_k[j], b_k_a[j])
            r, v, kk, gate, lw, kd, b = _rwkv_features(lay, xs, mod, norm1_g[l], p, seg_sum)
            y_f, y_b = _rwkv_scan(r, v, kk, lw, kd, b, n_batch=n_batch, n_ctx=n_ctx)
            o = _rwkv_out(lay, y_f, y_b, r, v, gate, kd, b_r_k[j], b_ln_w[j], b_ln_b[j], seg_mean)
            w_o = b_w_o[j]
        else:
            q, k, v = _qkv_project(lay, xs, mod, norm1_g[l], c_w_qkv[j], rope, no_gain, no_gain, seg_mean,
                                   n_kv_heads=2, qk_norm=False)
            o = _attention(lay, q, k, v, c_sink[j], n_kv_heads=2, window=True)
            w_o = c_w_o[j]
        w_route = jnp.zeros((d, LANES), F32).at[:, :N_GROUPS].set(moe_w_group[l]) \
            .at[:, N_GROUPS:N_GROUPS + N_EXPERTS].set(moe_w_expert[l])
        b_route = jnp.zeros((1, LANES), F32).at[0, :N_GROUPS].set(moe_b_group[l]) \
            .at[0, N_GROUPS:N_GROUPS + N_EXPERTS].set(moe_b_expert[l])
        xn, h2, route, counts = _post_mixer(lay, o, w_o, xs, mod, norm2_g[l], w_route, b_route)
        xs = _moe(lay, xn, h2, route, counts, mod, w1_all, w3_all, w2_all, l)
    return _final_norm(lay, xs, final_g)
```

```python
import functools
import math

import jax
import jax.numpy as jnp
from jax import lax
from jax.experimental import pallas as pl
from jax.experimental.pallas import tpu as pltpu

F32 = jnp.float32
BF16 = jnp.bfloat16
HI = lax.Precision.HIGHEST

HEAD = 64
LANES = 128
TM = 256
ATT_TQ = 128
ATT_TK = 2048
WINDOW = 128
SCAN_CHUNK = 64
MOE_ROWS = 256
N_GROUPS = 4
EXPERTS_PER_GROUP = 8
N_EXPERTS = N_GROUPS * EXPERTS_PER_GROUP
ROPE_THETA = 10000.0
NEG_INF = -1e30
Q_SCALE = HEAD ** -0.5
NORM_EPS = 1e-6
LN_X_EPS = 64e-5
VMEM_LIMIT = 56 * 1024 * 1024


def _params(n_axes):
    return pltpu.CompilerParams(dimension_semantics=("arbitrary",) * n_axes, vmem_limit_bytes=VMEM_LIMIT)


def _dot(a, b, precision=None):
    return jnp.dot(a, b, preferred_element_type=F32, precision=precision)


def _dot_nt(a, b, precision=None):
    return lax.dot_general(a, b, (((1,), (1,)), ((), ())), preferred_element_type=F32, precision=precision)


def _dot_tn(a, b, precision=None):
    return lax.dot_general(a, b, (((0,), (0,)), ((), ())), preferred_element_type=F32, precision=precision)


def _sigmoid(z):
    return 1.0 / (1.0 + jnp.exp(-z))


def _modulate(x, g, shift, scale):
    y = x * lax.rsqrt(jnp.mean(x * x, axis=-1, keepdims=True) + NORM_EPS)
    return (y * g) * (1.0 + scale) + shift


def _seg_apply(t, seg):
    seg2 = jnp.concatenate([seg, seg], axis=0).astype(BF16)
    t_hi = t.astype(BF16)
    t_lo = (t - t_hi.astype(F32)).astype(BF16)
    n = t.shape[1] // LANES
    tiles = [jnp.concatenate([t_hi[:, i * LANES:(i + 1) * LANES], t_lo[:, i * LANES:(i + 1) * LANES]], axis=1)
             for i in range(n)]
    return jnp.concatenate([_dot(tile, seg2) for tile in tiles], axis=1)


def _mod_kernel(c_ref, w_ref, b_ref, o_ref):
    cond = c_ref[...]
    o_ref[0] = _dot(cond * _sigmoid(cond), w_ref[0], HI) + b_ref[0]


def _mod_vectors(cond, mod_w, mod_b):
    depth, d, n = mod_w.shape
    tn = 1536
    return pl.pallas_call(
        _mod_kernel,
        grid=(depth, n // tn),
        in_specs=[pl.BlockSpec(cond.shape, lambda l, j: (0, 0)),
                  pl.BlockSpec((1, d, tn), lambda l, j: (l, 0, j)),
                  pl.BlockSpec((1, 1, tn), lambda l, j: (l, 0, j))],
        out_specs=pl.BlockSpec((1, cond.shape[0], tn), lambda l, j: (l, 0, j)),
        out_shape=jax.ShapeDtypeStruct((depth, cond.shape[0], n), F32),
        compiler_params=_params(2),
        name="mod_vectors",
    )(cond, mod_w, mod_b.reshape(depth, 1, n))


class _Layout:
    def __init__(self, n_batch, n_ctx, seq):
        assert n_ctx == TM and seq % TM == 0
        self.n_batch, self.n_ctx, self.seq = n_batch, n_ctx, seq
        self.per_batch = n_ctx + seq
        self.blocks_per_batch = self.per_batch // TM
        self.n_blocks = n_batch * self.blocks_per_batch
        self.t_all = n_batch * self.per_batch

    def mod_index(self, j):
        jb = j % self.blocks_per_batch
        return jnp.where(jb == 0, self.n_batch, j // self.blocks_per_batch)

    def mod_spec(self, d):
        return pl.BlockSpec((1, 6, d), lambda j: (self.mod_index(j), 0, 0))

    def row_spec(self, width):
        return pl.BlockSpec((TM, width), lambda j: (j, 0))

    def pos_spec(self, width):
        return pl.BlockSpec((TM, width), lambda j: (j % self.blocks_per_batch, 0))


def _full(shape):
    return pl.BlockSpec(shape, lambda *_: (0,) * len(shape))


def _qkv_kernel(x_ref, mod_ref, g_ref, w_ref, cos_ref, sina_ref, sinb_ref, qg_ref, kg_ref, seg_ref,
                q_ref, k_ref, v_ref, *, n_q, n_kv, qk_norm):
    h = _modulate(x_ref[...], g_ref[...], mod_ref[0, 0:1], mod_ref[0, 1:2])
    y = _dot(h.astype(BF16), w_ref[...])
    cos, sina, sinb = cos_ref[...], sina_ref[...], sinb_ref[...]
    seg = seg_ref[...]
    lane = lax.broadcasted_iota(jnp.int32, (TM, LANES), 1)
    low = lane < HEAD
    ones_col = jnp.where(lane == HEAD, 1.0, 0.0)

    def head_pair(t, gain):
        if qk_norm:
            t = t * lax.rsqrt(_seg_apply(t * t, seg) + NORM_EPS) * gain
        return t * cos + pltpu.roll(t, LANES - 16, 1) * sina + pltpu.roll(t, 16, 1) * sinb

    def both_heads(t):
        return t, pltpu.roll(t, HEAD, 1)

    for i in range(n_q // LANES):
        t = head_pair(y[:, i * LANES:(i + 1) * LANES], qg_ref[...]) * Q_SCALE
        for u, th in enumerate(both_heads(t)):
            q_ref[:, (2 * i + u) * LANES:(2 * i + u + 1) * LANES] = th.astype(BF16)
    for i in range(n_kv // LANES):
        t = head_pair(y[:, n_q + i * LANES:n_q + (i + 1) * LANES], kg_ref[...])
        for u, th in enumerate(both_heads(t)):
            k_ref[:, (2 * i + u) * LANES:(2 * i + u + 1) * LANES] = jnp.where(low, th, 0.0).astype(BF16)
        t = y[:, n_q + n_kv + i * LANES:n_q + n_kv + (i + 1) * LANES]
        for u, th in enumerate(both_heads(t)):
            v_ref[:, (2 * i + u) * LANES:(2 * i + u + 1) * LANES] = jnp.where(low, th, ones_col).astype(BF16)


def _qkv_project(lay, x, mod, g, w, rope, q_gain, k_gain, seg_mean, *, n_kv_heads, qk_norm):
    d = x.shape[1]
    n_q, n_kv = d, n_kv_heads * HEAD
    cos, sina, sinb = rope
    pair = lambda v: jnp.tile(v.reshape(1, HEAD), (1, LANES // HEAD))
    return pl.pallas_call(
        functools.partial(_qkv_kernel, n_q=n_q, n_kv=n_kv, qk_norm=qk_norm),
        grid=(lay.n_blocks,),
        in_specs=[lay.row_spec(d), lay.mod_spec(d), _full((1, d)), _full(w.shape),
                  lay.pos_spec(LANES), lay.pos_spec(LANES), lay.pos_spec(LANES),
                  _full((1, LANES)), _full((1, LANES)), _full((LANES, LANES))],
        out_specs=[lay.row_spec(2 * n_q), lay.row_spec(2 * n_kv), lay.row_spec(2 * n_kv)],
        out_shape=[jax.ShapeDtypeStruct((lay.t_all, 2 * n_q), BF16),
                   jax.ShapeDtypeStruct((lay.t_all, 2 * n_kv), BF16),
                   jax.ShapeDtypeStruct((lay.t_all, 2 * n_kv), BF16)],
        compiler_params=_params(1),
        name="qkv_project",
    )(x, mod, g.reshape(1, d), w.astype(BF16), cos, sina, sinb, pair(q_gain), pair(k_gain), seg_mean)


def _rope_tables(lay):
    rows = lay.seq // 64
    grid_w = 64
    row = jnp.repeat(jnp.arange(rows, dtype=F32), grid_w)
    col = jnp.tile(jnp.arange(grid_w, dtype=F32), rows)
    axis_dim = HEAD // 2
    inv_freq = ROPE_THETA ** (-jnp.arange(0, axis_dim, 2, dtype=F32) / axis_dim)
    a_row = row[:, None] * inv_freq
    a_col = col[:, None] * inv_freq
    ang = jnp.concatenate([a_row, a_row, a_col, a_col], axis=-1)
    cos = jnp.concatenate([jnp.ones((lay.n_ctx, HEAD), F32), jnp.cos(ang)], axis=0)
    sin = jnp.concatenate([jnp.zeros((lay.n_ctx, HEAD), F32), jnp.sin(ang)], axis=0)
    first = (jnp.arange(HEAD) % 32) < 16
    sina = jnp.where(first, -sin, 0.0)
    sinb = jnp.where(first, 0.0, sin)
    wide = lambda t: jnp.tile(t, (1, LANES // HEAD))
    return wide(cos), wide(sina), wide(sinb)


def _attn_kernel(sink_ref, q_ref, k_ref, v_ref, o_ref, *, n_kv, grp, window, n_ctx, seq):
    tq = ATT_TQ
    j = pl.program_id(1)
    rows = grp * tq
    ctx_blocks = n_ctx // tq
    gs = range(n_kv)
    tile = lambda ref, r0, g: ref[r0, g * LANES:(g + 1) * LANES]
    qg = [jnp.concatenate([tile(q_ref, slice(None), g * grp + u) for u in range(grp)], axis=0) for g in gs]

    def step(carry, ks, vs, keep=None):
        s = [_dot_nt(qg[g], ks[g]) for g in gs]
        if keep is not None:
            s = [jnp.where(keep, s[g], NEG_INF) for g in gs]
        m_new = [jnp.maximum(carry[g][0], jnp.max(s[g], axis=1, keepdims=True)) for g in gs]
        p = [jnp.exp(s[g] - m_new[g]).astype(BF16) for g in gs]
        alpha = [jnp.exp(carry[g][0] - m_new[g]) for g in gs]
        pv = [_dot(p[g], vs[g]) for g in gs]
        return tuple((m_new[g], alpha[g] * carry[g][1] + pv[g]) for g in gs)

    if window:
        lane = lax.broadcasted_iota(jnp.int32, (rows, LANES), 1)
        acc0 = jnp.where(lane == HEAD, 1.0, 0.0)
        carry = tuple((jnp.concatenate([jnp.full((tq, 1), sink_ref[g * grp + u], F32) for u in range(grp)],
                                       axis=0), acc0) for g in gs)
        tk = WINDOW
        rr = lax.broadcasted_iota(jnp.int32, (rows, tk), 0) & (tq - 1)
        cc = lax.broadcasted_iota(jnp.int32, (rows, tk), 1)
        j_lat = j - ctx_blocks
        n_lat = seq // tq
        never = 4 * tk
        for cb in range(n_ctx // tk):
            r0 = slice(cb * tk, (cb + 1) * tk)
            carry = step(carry, [tile(k_ref, r0, g) for g in gs], [tile(v_ref, r0, g) for g in gs])
        for off in (-1, 0, 1):
            blk = j_lat + off
            valid = jnp.logical_and(j_lat >= 0, jnp.logical_and(blk >= 0, blk < n_lat))
            r0 = pl.ds(pl.multiple_of(n_ctx + jnp.clip(blk, 0, n_lat - 1) * tk, tk), tk)
            bar = jnp.where(valid, 0, never)
            keep = {-1: cc >= rr + bar, 0: cc >= bar, 1: cc + bar <= rr}[off]
            carry = step(carry, [tile(k_ref, r0, g) for g in gs], [tile(v_ref, r0, g) for g in gs], keep)
    else:
        tk = min(ATT_TK, seq)
        assert seq % tk == 0
        carry = tuple((jnp.full((rows, 1), NEG_INF, F32), jnp.zeros((rows, LANES), F32)) for g in gs)
        ctx_rows = slice(0, n_ctx)
        carry = step(carry, [tile(k_ref, ctx_rows, g) for g in gs], [tile(v_ref, ctx_rows, g) for g in gs])
        n_tiles = jnp.where(j < ctx_blocks, 0, seq // tk)

        def body(t, c):
            r0 = pl.ds(pl.multiple_of(n_ctx + t * tk, math.gcd(n_ctx, tk)), tk)
            return step(c, [tile(k_ref, r0, g) for g in gs], [tile(v_ref, r0, g) for g in gs])

        carry = lax.fori_loop(0, n_tiles, body, carry)
    for g in gs:
        acc = carry[g][1]
        out = acc[:, :HEAD] / acc[:, HEAD:HEAD + 1]
        for u in range(grp):
            hd = g * grp + u
            o_ref[:, hd * HEAD:(hd + 1) * HEAD] = out[u * tq:(u + 1) * tq].astype(BF16)


def _attention(lay, q, k, v, sink, *, n_kv_heads, window):
    d = q.shape[1] // 2
    n_kv = n_kv_heads * LANES
    per_q = lay.per_batch // ATT_TQ
    kv_spec = pl.BlockSpec((lay.per_batch, n_kv), lambda b, j, *_: (b, 0), pipeline_mode=pl.Buffered(1))
    q_spec = pl.BlockSpec((ATT_TQ, 2 * d), lambda b, j, *_: (b * per_q + j, 0))
    o_spec = pl.BlockSpec((ATT_TQ, d), lambda b, j, *_: (b * per_q + j, 0))
    return pl.pallas_call(
        functools.partial(_attn_kernel, n_kv=n_kv_heads, grp=d // HEAD // n_kv_heads, window=window,
                          n_ctx=lay.n_ctx, seq=lay.seq),
        grid_spec=pltpu.PrefetchScalarGridSpec(
            num_scalar_prefetch=1, grid=(lay.n_batch, per_q),
            in_specs=[q_spec, kv_spec, kv_spec], out_specs=o_spec),
        out_shape=jax.ShapeDtypeStruct((lay.t_all, d), BF16),
        compiler_params=_params(2),
        name="window_attention" if window else "dense_attention",
    )(sink, q, k, v)


def _route(logits):
    lane = lax.broadcasted_iota(jnp.int32, logits.shape, 1)
    far = 4 * LANES
    is_g = lane < N_GROUPS
    gmax = jnp.max(jnp.where(is_g, logits, NEG_INF), axis=1, keepdims=True)
    gidx = jnp.min(jnp.where(jnp.logical_and(is_g, logits == gmax), lane, far), axis=1, keepdims=True)
    gsum = jnp.sum(jnp.where(is_g, jnp.exp(logits - gmax), 0.0), axis=1, keepdims=True)
    g_w = 1.0 / gsum
    is_e = jnp.logical_and(lane >= N_GROUPS, ((lane - N_GROUPS) >> 3) == gidx)
    emax = jnp.max(jnp.where(is_e, logits, NEG_INF), axis=1, keepdims=True)
    ee = jnp.where(is_e, jnp.exp(logits - emax), 0.0)
    p = ee / jnp.sum(ee, axis=1, keepdims=True)
    p_in = jnp.where(is_e, p, -1.0)
    p1 = jnp.max(p_in, axis=1, keepdims=True)
    i1 = jnp.min(jnp.where(p_in == p1, lane, far), axis=1, keepdims=True)
    p_rest = jnp.where(lane == i1, -1.0, p_in)
    p2 = jnp.max(p_rest, axis=1, keepdims=True)
    i2 = jnp.min(jnp.where(p_rest == p2, lane, far), axis=1, keepdims=True)
    tot = p1 + p2
    w1 = g_w * p1 / tot
    w2 = g_w * p2 / tot
    e1 = (i1 - N_GROUPS).astype(F32)
    e2 = (i2 - N_GROUPS).astype(F32)
    picks = (jnp.where(lane == i1, 1.0, 0.0), jnp.where(lane == i2, 1.0, 0.0))
    return (e1, e2, w1, w2), picks


def _post_kernel(o_ref, wo_ref, x_ref, mod_ref, g2_ref, wr_hi_ref, wr_lo_ref, br_ref,
                 xn_ref, h2_ref, route_ref, count_ref, seen_ref):
    @pl.when(pl.program_id(0) == 0)
    def _():
        seen_ref[...] = jnp.zeros_like(seen_ref)

    y = _dot(o_ref[...], wo_ref[...])
    xn = x_ref[...] + mod_ref[0, 2:3] * y
    xn_ref[...] = xn
    h2 = _modulate(xn, g2_ref[...], mod_ref[0, 3:4], mod_ref[0, 4:5])
    h2_ref[...] = h2
    h_hi = _bf(h2)
    h_lo = _bf(h2 - h_hi.astype(F32))
    logits = _dot(h_hi, wr_hi_ref[...]) + (_dot(h_hi, wr_lo_ref[...]) + _dot(h_lo, wr_hi_ref[...]))
    cols, (pick1, pick2) = _route(logits + br_ref[...])
    both = pick1 + pick2
    row = lax.broadcasted_iota(jnp.int32, (TM, TM), 0)
    col = lax.broadcasted_iota(jnp.int32, (TM, TM), 1)
    earlier = jnp.where(col < row, 1.0, 0.0).astype(BF16)
    before = _dot(earlier, both.astype(BF16)) + seen_ref[0:1]
    ranks = (jnp.sum(before * pick1, axis=1, keepdims=True), jnp.sum(before * pick2, axis=1, keepdims=True))
    seen = seen_ref[0:1] + jnp.sum(both, axis=0, keepdims=True)
    seen_ref[...] = jnp.broadcast_to(seen, seen_ref.shape)
    count_ref[...] = jnp.broadcast_to(seen, count_ref.shape)
    lane = lax.broadcasted_iota(jnp.int32, (TM, LANES), 1)
    out = jnp.zeros((TM, LANES), F32)
    for i, c in enumerate(cols + ranks):
        out = jnp.where(lane == i, c, out)
    route_ref[...] = out


def _post_mixer(lay, o, w_o, x, mod, g2, w_route, b_route):
    d = x.shape[1]
    w_hi = w_route.astype(BF16)
    return pl.pallas_call(
        _post_kernel,
        grid=(lay.n_blocks,),
        in_specs=[lay.row_spec(d), _full((d, d)), lay.row_spec(d), lay.mod_spec(d), _full((1, d)),
                  _full((d, LANES)), _full((d, LANES)), _full((1, LANES))],
        out_specs=[lay.row_spec(d), lay.row_spec(d), lay.row_spec(LANES), _full((8, LANES))],
        out_shape=[jax.ShapeDtypeStruct((lay.t_all, d), F32), jax.ShapeDtypeStruct((lay.t_all, d), F32),
                   jax.ShapeDtypeStruct((lay.t_all, LANES), F32), jax.ShapeDtypeStruct((8, LANES), F32)],
        scratch_shapes=[pltpu.VMEM((8, LANES), F32)],
        compiler_params=_params(1),
        name="post_mixer",
    )(o, w_o.astype(BF16), x, mod, g2.reshape(1, d), w_hi, (w_route - w_hi.astype(F32)).astype(BF16), b_route)


def _dispatch_kernel(dest_ref, h_ref, xs_in, xs_hbm, sem):
    del xs_in

    def row_copy(r, k):
        return pltpu.make_async_copy(h_ref.at[pl.ds(r, 1)], xs_hbm.at[pl.ds(dest_ref[0, k, r], 1)], sem)

    def start(r, carry):
        row_copy(r, 0).start()
        row_copy(r, 1).start()
        return carry

    def wait(r, carry):
        row_copy(r, 0).wait()
        row_copy(r, 1).wait()
        return carry

    lax.fori_loop(0, TM, start, 0, unroll=8)
    lax.fori_loop(0, TM, wait, 0, unroll=8)


def _dispatch(lay, h2, dest_blocks, n_rows):
    d = h2.shape[1]
    return pl.pallas_call(
        _dispatch_kernel,
        grid=(lay.n_blocks,),
        in_specs=[pl.BlockSpec((1, 2, TM), lambda j: (j, 0, 0), memory_space=pltpu.SMEM), lay.row_spec(d),
                  pl.BlockSpec(memory_space=pl.ANY)],
        out_specs=pl.BlockSpec(memory_space=pl.ANY),
        out_shape=jax.ShapeDtypeStruct((n_rows, d), F32),
        scratch_shapes=[pltpu.SemaphoreType.DMA(())],
        input_output_aliases={2: 0},
        compiler_params=_params(1),
        name="moe_dispatch",
    )(dest_blocks, h2, jnp.zeros((n_rows, d), F32))


def _expert_kernel(blk_ref, used_ref, x_ref, w1_ref, w3_ref, w2_ref, out_ref):
    del blk_ref
    i = pl.program_id(0)

    @pl.when(i < used_ref[0])
    def _():
        xb = x_ref[...].astype(BF16)
        a = _dot(xb, w1_ref[0].astype(BF16))
        hid = (a * _sigmoid(a)) * _dot(xb, w3_ref[0].astype(BF16))
        out_ref[...] = _dot(hid.astype(BF16), w2_ref[0].astype(BF16))

    @pl.when(i >= used_ref[0])
    def _():
        out_ref[...] = jnp.zeros_like(out_ref)


def _expert_blocks(xs, blk_expert, n_used, w1, w3, w2, layer):
    n_rows, d = xs.shape
    n_blocks = n_rows // MOE_ROWS
    de = w1.shape[2]
    base = layer * N_EXPERTS
    return pl.pallas_call(
        _expert_kernel,
        grid_spec=pltpu.PrefetchScalarGridSpec(
            num_scalar_prefetch=2, grid=(n_blocks,),
            in_specs=[pl.BlockSpec((MOE_ROWS, d), lambda i, blk, used: (i, 0)),
                      pl.BlockSpec((1, d, de), lambda i, blk, used: (base + blk[i], 0, 0)),
                      pl.BlockSpec((1, d, de), lambda i, blk, used: (base + blk[i], 0, 0)),
                      pl.BlockSpec((1, de, d), lambda i, blk, used: (base + blk[i], 0, 0))],
            out_specs=pl.BlockSpec((MOE_ROWS, d), lambda i, blk, used: (i, 0))),
        out_shape=jax.ShapeDtypeStruct((n_rows, d), F32),
        compiler_params=_params(1),
        name="expert_blocks",
    )(blk_expert, n_used, xs, w1, w3, w2)


def _combine_kernel(pos_ref, pos_next_ref, ys_hbm, x_ref, mod_ref, route_ref, o_ref, buf, sem):
    j = pl.program_id(0)
    last = pl.num_programs(0) - 1
    slot = j % 2

    def row_copy(p_ref, r, k, s):
        return pltpu.make_async_copy(ys_hbm.at[pl.ds(p_ref[0, k, r], 1)], buf.at[s, k, pl.ds(r, 1)], sem.at[s])

    def issue(p_ref, s):
        def start(r, carry):
            row_copy(p_ref, r, 0, s).start()
            row_copy(p_ref, r, 1, s).start()
            return carry
        lax.fori_loop(0, TM, start, 0, unroll=8)

    @pl.when(j == 0)
    def _():
        issue(pos_ref, 0)

    @pl.when(j < last)
    def _():
        issue(pos_next_ref, 1 - slot)

    def wait(r, carry):
        row_copy(pos_ref, r, 0, slot).wait()
        row_copy(pos_ref, r, 1, slot).wait()
        return carry

    lax.fori_loop(0, TM, wait, 0, unroll=8)
    route = route_ref[...]
    f = buf[slot, 0] * route[:, 2:3] + buf[slot, 1] * route[:, 3:4]
    o_ref[...] = x_ref[...] + mod_ref[0, 5:6] * f


def _combine(lay, ys, pos, x, mod, route):
    d = x.shape[1]
    pos_spec = lambda idx: pl.BlockSpec((1, 2, TM), lambda j: (idx(j), 0, 0), memory_space=pltpu.SMEM)
    return pl.pallas_call(
        _combine_kernel,
        grid=(lay.n_blocks,),
        in_specs=[pos_spec(lambda j: j), pos_spec(lambda j: jnp.minimum(j + 1, lay.n_blocks - 1)),
                  pl.BlockSpec(memory_space=pl.ANY), lay.row_spec(d), lay.mod_spec(d), lay.row_spec(LANES)],
        out_specs=lay.row_spec(d),
        out_shape=jax.ShapeDtypeStruct((lay.t_all, d), F32),
        scratch_shapes=[pltpu.VMEM((2, 2, TM, d), F32), pltpu.SemaphoreType.DMA((2,))],
        compiler_params=_params(1),
        name="moe_combine",
    )(pos, pos, ys, x, mod, route)


def _dispatch_plan(route, counts):
    n_tok = route.shape[0]
    n_blocks = -(-(n_tok * 2) // MOE_ROWS) + N_EXPERTS
    counts = counts[0, N_GROUPS:N_GROUPS + N_EXPERTS].astype(jnp.int32)
    padded = (counts + MOE_ROWS - 1) // MOE_ROWS * MOE_ROWS
    p_end = jnp.cumsum(padded)
    first_slot = p_end - padded
    expert = route[:, 0:2].astype(jnp.int32)
    onehot = expert[:, :, None] == jnp.arange(N_EXPERTS, dtype=jnp.int32)
    dest = jnp.sum(jnp.where(onehot, first_slot, 0), axis=-1) + route[:, 4:6].astype(jnp.int32)
    block_start = jnp.arange(n_blocks, dtype=jnp.int32) * MOE_ROWS
    blk_expert = jnp.minimum(jnp.sum(p_end[None, :] <= block_start[:, None], axis=1), N_EXPERTS - 1).astype(jnp.int32)
    n_used = (p_end[-1:] // MOE_ROWS).astype(jnp.int32)
    return dest, blk_expert, n_used, n_blocks


def _moe(lay, xn, h2, route, counts, mod, w1, w3, w2, layer):
    dest, blk_expert, n_used, n_blocks = _dispatch_plan(route, counts)
    dest_blocks = dest.reshape(lay.n_blocks, TM, 2).transpose(0, 2, 1)
    xs = _dispatch(lay, h2, dest_blocks, n_blocks * MOE_ROWS)
    ys = _expert_blocks(xs, blk_expert, n_used, w1, w3, w2, layer)
    return _combine(lay, ys, dest_blocks, xn, mod, route)


def _rwkv_feat_kernel(x_ref, xp_ref, xq_ref, mod_ref, g_ref, mu_ref, wr_ref, wk_ref, wv_ref, w0_ref, w1_ref, w2_ref,
                      a0_ref, a1_ref, a2_ref, g1_ref, g2_ref, kk_ref, ka_ref, seg_ref,
                      r_ref, v_ref, kn_ref, gate_ref, lw_ref, kd_ref, b_ref, *, blocks_per_batch):
    jb = pl.program_id(0) % blocks_per_batch
    shift, scale, g = mod_ref[0, 0:1], mod_ref[0, 1:2], g_ref[...]
    h = _modulate(x_ref[...], g, shift, scale)
    has_prev = jnp.where(jb > 1, 1.0, 0.0)
    has_next = jnp.where(jnp.logical_and(jb > 0, jb < blocks_per_batch - 1), 1.0, 0.0)
    h_before = _modulate(xp_ref[...], g, shift, scale)[7:8] * has_prev
    h_after = _modulate(xq_ref[...], g, shift, scale)[0:1] * has_next
    rows = lax.broadcasted_iota(jnp.int32, h.shape, 0)
    prev = jnp.where(rows == 0, h_before, pltpu.roll(h, 1, 0))
    nxt = jnp.where(rows == TM - 1, h_after, pltpu.roll(h, TM - 1, 0))
    xx = 0.5 * (prev + nxt) - h
    mix = lambda i: (h + xx * mu_ref[i:i + 1]).astype(BF16)
    xr, xw, xk, xv, xa, xg = [mix(i) for i in range(6)]
    r = _dot(xr, wr_ref[...])
    k = _dot(xk, wk_ref[...])
    r_ref[...] = r
    v_ref[...] = _dot(xv, wv_ref[...])
    gate_ref[...] = _dot(_sigmoid(_dot(xg, g1_ref[...])).astype(BF16), g2_ref[...])
    kk = k * kk_ref[...]
    norm = jnp.sqrt(_seg_apply(kk * kk, seg_ref[...]))
    kk = kk / jnp.maximum(norm, 1e-12)
    kn_ref[...] = kk
    for n in range(2):
        w_log = w0_ref[n:n + 1] + _dot(jnp.tanh(_dot(xw, w1_ref[n])).astype(BF16), w2_ref[n])
        z = -w_log
        softplus = jnp.maximum(z, 0.0) + jnp.log(1.0 + jnp.exp(-jnp.abs(z)))
        lw_ref[n] = -jnp.exp(-softplus - 0.5)
        iclr = _sigmoid(a0_ref[n:n + 1] + _dot(_dot(xa, a1_ref[n]).astype(BF16), a2_ref[n]))
        kd_ref[n] = k * (1.0 + (iclr - 1.0) * ka_ref[...])
        b_ref[n] = kk * iclr


def _rwkv_features(lay, x, mod, g, p, seg_sum):
    (mu, w_r, w_k, w_v, w0, w1, w2, a0, a1, a2, g1, g2, k_k, k_a) = p
    d = x.shape[1]
    row = lay.row_spec(d)
    sub = TM // 8
    last = lay.t_all // 8 - 1
    before = pl.BlockSpec((8, d), lambda j: (jnp.maximum(j * sub - 1, 0), 0))
    after = pl.BlockSpec((8, d), lambda j: (jnp.minimum((j + 1) * sub, last), 0))
    both = pl.BlockSpec((2, TM, d), lambda j: (0, j, 0))
    bf = lambda t: t.astype(BF16)
    args = (x, x, x, mod, g.reshape(1, d), mu, bf(w_r), bf(w_k), bf(w_v), w0, bf(w1), bf(w2), a0, bf(a1), bf(a2),
            bf(g1), bf(g2), k_k.reshape(1, d), k_a.reshape(1, d), seg_sum)
    in_specs = [row, before, after, lay.mod_spec(d)] + [_full(a.shape) for a in args[4:]]
    shape = jax.ShapeDtypeStruct((lay.t_all, d), F32)
    shape2 = jax.ShapeDtypeStruct((2, lay.t_all, d), F32)
    return pl.pallas_call(
        functools.partial(_rwkv_feat_kernel, blocks_per_batch=lay.blocks_per_batch),
        grid=(lay.n_blocks,),
        in_specs=in_specs,
        out_specs=[row, row, row, row, both, both, both],
        out_shape=[shape, shape, shape, shape, shape2, shape2, shape2],
        compiler_params=_params(1),
        name="rwkv_features",
    )(*args)


def _scan_operands(sign, r_ref, v_ref, kk_ref, lw_ref, kd_ref, b_ref, n_heads):
    c = SCAN_CHUNK
    row = lax.broadcasted_iota(jnp.int32, (c, c), 0)
    col = lax.broadcasted_iota(jnp.int32, (c, c), 1)
    incl = (row - col) * sign >= 0
    lw = lw_ref[0]
    tri = jnp.where(incl, 1.0, 0.0).astype(BF16)
    lw_hi = _bf(lw)
    lw_lo = _bf(lw - lw_hi.astype(F32))
    lg = _dot(jnp.concatenate([tri, tri], axis=1), jnp.concatenate([lw_hi, lw_lo], axis=0))
    lg_tot = jnp.sum(lw, axis=0, keepdims=True)
    g_inv = jnp.exp(-lg)
    g_rem = jnp.exp(lg_tot - lg)
    g_tot = jnp.exp(lg_tot)
    v_all = v_ref[...]
    even = (lax.broadcasted_iota(jnp.int32, lw.shape, 1) & HEAD) == 0
    halves = lambda t: (_bf(jnp.where(even, t, 0.0)), _bf(jnp.where(even, 0.0, t)))
    ar_all = _bf(jnp.concatenate([-kk_ref[...] * jnp.exp(lg - lw), r_ref[...] * jnp.exp(lg)], axis=0))
    b_t, k_t = halves(b_ref[0] * g_inv), halves(kd_ref[0] * g_inv)
    b_h, k_h = halves(b_ref[0] * g_rem), halves(kd_ref[0] * g_rem)

    lag2 = (lax.broadcasted_iota(jnp.int32, (c, LANES), 0)
            - (lax.broadcasted_iota(jnp.int32, (c, LANES), 1) & (HEAD - 1))) * sign
    hs = range(n_heads)
    pair = [slice(h // 2 * LANES, (h // 2 + 1) * LANES) for h in hs]
    return dict(
        strict2=[lag2 > 0] * n_heads,
        incl2=[lag2 >= 0] * n_heads,
        ar=[ar_all[:, pair[h]] for h in hs],
        bk=[jnp.concatenate([b_t[h % 2][:, pair[h]], k_t[h % 2][:, pair[h]], b_t[h % 2][:, pair[h]],
                             b_t[h % 2][:, pair[h]]], axis=0) for h in hs],
        bkh=[jnp.concatenate([b_h[h % 2][:, pair[h]], k_h[h % 2][:, pair[h]]], axis=0) for h in hs],
        v_h=[v_all[:, h * HEAD:(h + 1) * HEAD] for h in hs],
        g_tot=[g_tot[:, pair[h]] for h in hs])


def _scan_kernel(rf_ref, vf_ref, kkf_ref, lwf_ref, kdf_ref, bf_ref, rb_ref, vb_ref, kkb_ref, lwb_ref, kdb_ref, bb_ref,
                 yf_ref, yb_ref, s_ref, *, n_heads):
    c = SCAN_CHUNK

    @pl.when(pl.program_id(1) == 0)
    def _():
        s_ref[...] = jnp.zeros_like(s_ref)

    fwd = _scan_operands(1, rf_ref, vf_ref, kkf_ref, lwf_ref, kdf_ref, bf_ref, n_heads)
    bwd = _scan_operands(-1, rb_ref, vb_ref, kkb_ref, lwb_ref, kdb_ref, bb_ref, n_heads)
    op = {key: fwd[key] + bwd[key] for key in fwd}
    hs = range(2 * n_heads)
    zeros_v = jnp.zeros((c, HEAD), BF16)
    zeros_l = jnp.zeros((c, LANES), BF16)
    s0 = [s_ref[h] for h in hs]
    p = [_dot_nt(op["ar"][h], op["bk"][h]) for h in hs]
    ars = [_dot_nt(op["ar"][h], _bf(s0[h])) for h in hs]
    x = [ars[h][:c] + _dot(_bf(jnp.where(op["strict2"][h], p[h][:c, :LANES], 0.0)),
                           jnp.concatenate([zeros_v, _bf(op["v_h"][h])], axis=0)) for h in hs]
    pw = [_bf(jnp.where(op["strict2"][h], p[h][:c, LANES:], 0.0)) for h in hs]
    x = [x[h] + _apply_split(pw[h], x[h]) for h in hs]
    for _ in range(5):
        pw = [_bf(_dot(pw[h], jnp.concatenate([pw[h], zeros_l], axis=0))) for h in hs]
        x = [x[h] + _apply_split(pw[h], x[h]) for h in hs]
    uv = [_bf(jnp.concatenate([x[h], op["v_h"][h]], axis=0)) for h in hs]
    y = [ars[h][c:] + _dot(_bf(jnp.where(op["incl2"][h], p[h][c:, :LANES], 0.0)), uv[h]) for h in hs]
    yf_ref[...] = jnp.concatenate(y[:n_heads], axis=1)
    yb_ref[...] = jnp.concatenate(y[n_heads:], axis=1)
    states = [s0[h] * op["g_tot"][h] + _dot_tn(uv[h], op["bkh"][h]) for h in hs]
    for h in hs:
        s_ref[h] = states[h]


def _bf(t):
    return t.astype(BF16)


def _apply_split(l_dup, x):
    x_hi = _bf(x)
    x_lo = _bf(x - x_hi.astype(F32))
    return _dot(l_dup, jnp.concatenate([x_hi, x_lo], axis=0))


def _rwkv_scan(r, v, kk, lw, kd, b, *, n_batch, n_ctx):
    t_all, d = r.shape
    c = SCAN_CHUNK
    per_batch = t_all // n_batch // c
    ctx_chunks = n_ctx // c
    n_heads = d // HEAD

    fwd_chunk = lambda bi, i: bi * per_batch + i
    bwd_chunk = lambda bi, i: bi * per_batch + jnp.where(i < ctx_chunks, ctx_chunks - 1 - i,
                                                         per_batch - 1 - (i - ctx_chunks))
    specs = []
    for dr, chunk in ((0, fwd_chunk), (1, bwd_chunk)):
        shared = pl.BlockSpec((c, d), lambda bi, i, chunk=chunk: (chunk(bi, i), 0))
        per_dir = pl.BlockSpec((1, c, d), lambda bi, i, chunk=chunk, dr=dr: (dr, chunk(bi, i), 0))
        specs.append((shared, per_dir))
    in_specs = [sp for shared, per_dir in specs for sp in (shared, shared, shared, per_dir, per_dir, per_dir)]
    out = jax.ShapeDtypeStruct((t_all, d), F32)
    return pl.pallas_call(
        functools.partial(_scan_kernel, n_heads=n_heads),
        grid=(n_batch, per_batch),
        in_specs=in_specs,
        out_specs=[specs[0][0], specs[1][0]],
        out_shape=[out, out],
        scratch_shapes=[pltpu.VMEM((2 * n_heads, HEAD, LANES), F32)],
        compiler_params=_params(2),
        name="rwkv_scan",
    )(r, v, kk, lw, kd, b, r, v, kk, lw, kd, b)


def _rwkv_out_kernel(yf_ref, yb_ref, r_ref, v_ref, gate_ref, kd_ref, rk_ref, lnw_ref, lnb_ref, seg_ref, o_ref):
    seg = seg_ref[...]
    y = yf_ref[...] + yb_ref[...]
    mean = _seg_apply(y, seg)
    cen = y - mean
    var = _seg_apply(cen * cen, seg)
    yn = cen * lax.rsqrt(var + LN_X_EPS) * lnw_ref[...] + lnb_ref[...]
    bonus = _seg_apply(r_ref[...] * (kd_ref[0] + kd_ref[1]) * rk_ref[...], seg) * float(HEAD) * v_ref[...]
    o_ref[...] = ((yn + bonus) * gate_ref[...]).astype(BF16)


def _rwkv_out(lay, y_f, y_b, r, v, gate, kd, r_k, ln_w, ln_b, seg_mean):
    d = r.shape[1]
    row = lay.row_spec(d)
    both = pl.BlockSpec((2, TM, d), lambda j: (0, j, 0))
    return pl.pallas_call(
        _rwkv_out_kernel,
        grid=(lay.n_blocks,),
        in_specs=[row, row, row, row, row, both, _full((1, d)), _full((1, d)), _full((1, d)), _full((LANES, LANES))],
        out_specs=row,
        out_shape=jax.ShapeDtypeStruct((lay.t_all, d), BF16),
        compiler_params=_params(1),
        name="rwkv_out",
    )(y_f, y_b, r, v, gate, kd, r_k.reshape(1, d), ln_w.reshape(1, d), ln_b.reshape(1, d), seg_mean)


def _final_kernel(x_ref, g_ref, o_ref):
    x = x_ref[...]
    o_ref[0] = x * lax.rsqrt(jnp.mean(x * x, axis=-1, keepdims=True) + NORM_EPS) * g_ref[...]


def _final_norm(lay, x, g):
    d = x.shape[1]
    lat_blocks = lay.seq // TM
    return pl.pallas_call(
        _final_kernel,
        grid=(lay.n_batch, lat_blocks),
        in_specs=[pl.BlockSpec((TM, d), lambda b, j: (b * lay.blocks_per_batch + 1 + j, 0)), _full((1, d))],
        out_specs=pl.BlockSpec((1, TM, d), lambda b, j: (b, j, 0)),
        out_shape=jax.ShapeDtypeStruct((lay.n_batch, lay.seq, d), F32),
        compiler_params=_params(2),
        name="final_norm",
    )(x, g.reshape(1, d))


def kernel(x, c, ctx, c_ctx, mod_w, mod_b, norm1_g, norm2_g, a_w_qkv, a_w_o, a_q_gain, a_k_gain, b_mu, b_w_r, b_w_k, b_w_v, b_w_o, b_decay_w0, b_decay_w1, b_decay_w2, b_iclr_a0, b_iclr_a1, b_iclr_a2, b_gate_g1, b_gate_g2, b_k_k, b_k_a, b_r_k, b_ln_w, b_ln_b, c_w_qkv, c_w_o, c_sink, moe_w_group, moe_b_group, moe_w_expert, moe_b_expert, moe_w1, moe_w3, moe_w2, final_g):
    n_batch, seq, d = x.shape
    n_ctx = ctx.shape[1]
    depth = mod_w.shape[0]
    lay = _Layout(n_batch, n_ctx, seq)

    cond = jnp.zeros((8, d), F32).at[:n_batch].set(c).at[n_batch].set(c_ctx)
    mods = _mod_vectors(cond, mod_w, mod_b)[:, :n_batch + 1].reshape(depth, n_batch + 1, 6, d)
    xs = jnp.concatenate([ctx, x], axis=1).reshape(lay.t_all, d)

    blockdiag = (jnp.arange(LANES)[:, None] // HEAD == jnp.arange(LANES)[None, :] // HEAD).astype(F32)
    seg_sum, seg_mean = blockdiag, blockdiag / HEAD
    rope = _rope_tables(lay)
    no_gain = jnp.ones((HEAD,), F32)
    stacked = lambda w: w.reshape((-1,) + w.shape[2:])
    w1_all, w3_all, w2_all = stacked(moe_w1), stacked(moe_w3), stacked(moe_w2)
    no_sink = jnp.zeros((d // HEAD,), F32)

    for l in range(depth):
        kind, j = l % 3, l // 3
        mod = mods[l]
        if kind == 0:
            q, k, v = _qkv_project(lay, xs, mod, norm1_g[l], a_w_qkv[j], rope, a_q_gain[j], a_k_gain[j], seg_mean,
                                   n_kv_heads=4, qk_norm=True)
            o = _attention(lay, q, k, v, no_sink, n_kv_heads=4, window=False)
            w_o = a_w_o[j]
        elif kind == 1:
            p = (b_mu[j], b_w_r[j], b_w_k[j], b_w_v[j], b_decay_w0[j], b_decay_w1[j], b_decay_w2[j],
                 b_iclr_a0[j], b_iclr_a1[j], b_iclr_a2[j], b_gate_g1[j], b_gate_g2[j], b_k_k[j], b_k_a[j])
            r, v, kk, gate, lw, kd, b = _rwkv_features(lay, xs, mod, norm1_g[l], p, seg_sum)
            y_f, y_b = _rwkv_scan(r, v, kk, lw, kd, b, n_batch=n_batch, n_ctx=n_ctx)
            o = _rwkv_out(lay, y_f, y_b, r, v, gate, kd, b_r_k[j], b_ln_w[j], b_ln_b[j], seg_mean)
            w_o = b_w_o[j]
        else:
            q, k, v = _qkv_project(lay, xs, mod, norm1_g[l], c_w_qkv[j], rope, no_gain, no_gain, seg_mean,
                                   n_kv_heads=2, qk_norm=False)
            o = _attention(lay, q, k, v, c_sink[j], n_kv_heads=2, window=True)
            w_o = c_w_o[j]
        w_route = jnp.zeros((d, LANES), F32).at[:, :N_GROUPS].set(moe_w_group[l]) \
            .at[:, N_GROUPS:N_GROUPS + N_EXPERTS].set(moe_w_expert[l])
        b_route = jnp.zeros((1, LANES), F32).at[0, :N_GROUPS].set(moe_b_group[l]) \
            .at[0, N_GROUPS:N_GROUPS + N_EXPERTS].set(moe_b_expert[l])
        xn, h2, route, counts = _post_mixer(lay, o, w_o, xs, mod, norm2_g[l], w_route, b_route)
        xs = _moe(lay, xn, h2, route, counts, mod, w1_all, w3_all, w2_all, l)
    return _final_norm(lay, xs, final_g)
```

```python
import functools
import math

import jax
import jax.numpy as jnp
from jax import lax
from jax.experimental import pallas as pl
from jax.experimental.pallas import tpu as pltpu

F32 = jnp.float32
BF16 = jnp.bfloat16
HI = lax.Precision.HIGHEST

HEAD = 64
LANES = 128
TM = 256
ATT_TQ = 128
ATT_TK = 2048
WINDOW = 128
SCAN_CHUNK = 64
MOE_ROWS = 256
N_GROUPS = 4
EXPERTS_PER_GROUP = 8
N_EXPERTS = N_GROUPS * EXPERTS_PER_GROUP
ROPE_THETA = 10000.0
NEG_INF = -1e30
Q_SCALE = HEAD ** -0.5
NORM_EPS = 1e-6
LN_X_EPS = 64e-5
VMEM_LIMIT = 56 * 1024 * 1024


def _params(n_axes):
    return pltpu.CompilerParams(dimension_semantics=("arbitrary",) * n_axes, vmem_limit_bytes=VMEM_LIMIT)


def _dot(a, b, precision=None):
    return jnp.dot(a, b, preferred_element_type=F32, precision=precision)


def _dot_nt(a, b, precision=None):
    return lax.dot_general(a, b, (((1,), (1,)), ((), ())), preferred_element_type=F32, precision=precision)


def _dot_tn(a, b, precision=None):
    return lax.dot_general(a, b, (((0,), (0,)), ((), ())), preferred_element_type=F32, precision=precision)


def _sigmoid(z):
    return 1.0 / (1.0 + jnp.exp(-z))


def _modulate(x, g, shift, scale):
    y = x * lax.rsqrt(jnp.mean(x * x, axis=-1, keepdims=True) + NORM_EPS)
    return (y * g) * (1.0 + scale) + shift


def _seg_apply(t, seg):
    seg2 = jnp.concatenate([seg, seg], axis=0).astype(BF16)
    t_hi = t.astype(BF16)
    t_lo = (t - t_hi.astype(F32)).astype(BF16)
    n = t.shape[1] // LANES
    tiles = [jnp.concatenate([t_hi[:, i * LANES:(i + 1) * LANES], t_lo[:, i * LANES:(i + 1) * LANES]], axis=1)
             for i in range(n)]
    return jnp.concatenate([_dot(tile, seg2) for tile in tiles], axis=1)


def _mod_kernel(c_ref, w_ref, b_ref, o_ref):
    cond = c_ref[...]
    o_ref[0] = _dot(cond * _sigmoid(cond), w_ref[0], HI) + b_ref[0]


def _mod_vectors(cond, mod_w, mod_b):
    depth, d, n = mod_w.shape
    tn = 1536
    return pl.pallas_call(
        _mod_kernel,
        grid=(depth, n // tn),
        in_specs=[pl.BlockSpec(cond.shape, lambda l, j: (0, 0)),
                  pl.BlockSpec((1, d, tn), lambda l, j: (l, 0, j)),
                  pl.BlockSpec((1, 1, tn), lambda l, j: (l, 0, j))],
        out_specs=pl.BlockSpec((1, cond.shape[0], tn), lambda l, j: (l, 0, j)),
        out_shape=jax.ShapeDtypeStruct((depth, cond.shape[0], n), F32),
        compiler_params=_params(2),
        name="mod_vectors",
    )(cond, mod_w, mod_b.reshape(depth, 1, n))


class _Layout:
    def __init__(self, n_batch, n_ctx, seq):
        assert n_ctx == TM and seq % TM == 0
        self.n_batch, self.n_ctx, self.seq = n_batch, n_ctx, seq
        self.per_batch = n_ctx + seq
        self.blocks_per_batch = self.per_batch // TM
        self.n_blocks = n_batch * self.blocks_per_batch
        self.t_all = n_batch * self.per_batch

    def mod_index(self, j):
        jb = j % self.blocks_per_batch
        return jnp.where(jb == 0, self.n_batch, j // self.blocks_per_batch)

    def mod_spec(self, d):
        return pl.BlockSpec((1, 6, d), lambda j: (self.mod_index(j), 0, 0))

    def row_spec(self, width):
        return pl.BlockSpec((TM, width), lambda j: (j, 0))

    def pos_spec(self, width):
        return pl.BlockSpec((TM, width), lambda j: (j % self.blocks_per_batch, 0))


def _full(shape):
    return pl.BlockSpec(shape, lambda *_: (0,) * len(shape))


def _qkv_kernel(x_ref, mod_ref, g_ref, w_ref, cos_ref, sina_ref, sinb_ref, qg_ref, kg_ref, seg_ref,
                q_ref, k_ref, v_ref, *, n_q, n_kv, qk_norm):
    h = _modulate(x_ref[...], g_ref[...], mod_ref[0, 0:1], mod_ref[0, 1:2])
    y = _dot(h.astype(BF16), w_ref[...])
    cos, sina, sinb = cos_ref[...], sina_ref[...], sinb_ref[...]
    seg = seg_ref[...]
    lane = lax.broadcasted_iota(jnp.int32, (TM, LANES), 1)
    low = lane < HEAD
    ones_col = jnp.where(lane == HEAD, 1.0, 0.0)

    def head_pair(t, gain):
        if qk_norm:
            t = t * lax.rsqrt(_seg_apply(t * t, seg) + NORM_EPS) * gain
        return t * cos + pltpu.roll(t, LANES - 16, 1) * sina + pltpu.roll(t, 16, 1) * sinb

    def both_heads(t):
        return t, pltpu.roll(t, HEAD, 1)

    for i in range(n_q // LANES):
        t = head_pair(y[:, i * LANES:(i + 1) * LANES], qg_ref[...]) * Q_SCALE
        for u, th in enumerate(both_heads(t)):
            q_ref[:, (2 * i + u) * LANES:(2 * i + u + 1) * LANES] = th.astype(BF16)
    for i in range(n_kv // LANES):
        t = head_pair(y[:, n_q + i * LANES:n_q + (i + 1) * LANES], kg_ref[...])
        for u, th in enumerate(both_heads(t)):
            k_ref[:, (2 * i + u) * LANES:(2 * i + u + 1) * LANES] = jnp.where(low, th, 0.0).astype(BF16)
        t = y[:, n_q + n_kv + i * LANES:n_q + n_kv + (i + 1) * LANES]
        for u, th in enumerate(both_heads(t)):
            v_ref[:, (2 * i + u) * LANES:(2 * i + u + 1) * LANES] = jnp.where(low, th, ones_col).astype(BF16)


def _qkv_project(lay, x, mod, g, w, rope, q_gain, k_gain, seg_mean, *, n_kv_heads, qk_norm):
    d = x.shape[1]
    n_q, n_kv = d, n_kv_heads * HEAD
    cos, sina, sinb = rope
    pair = lambda v: jnp.tile(v.reshape(1, HEAD), (1, LANES // HEAD))
    return pl.pallas_call(
        functools.partial(_qkv_kernel, n_q=n_q, n_kv=n_kv, qk_norm=qk_norm),
        grid=(lay.n_blocks,),
        in_specs=[lay.row_spec(d), lay.mod_spec(d), _full((1, d)), _full(w.shape),
                  lay.pos_spec(LANES), lay.pos_spec(LANES), lay.pos_spec(LANES),
                  _full((1, LANES)), _full((1, LANES)), _full((LANES, LANES))],
        out_specs=[lay.row_spec(2 * n_q), lay.row_spec(2 * n_kv), lay.row_spec(2 * n_kv)],
        out_shape=[jax.ShapeDtypeStruct((lay.t_all, 2 * n_q), BF16),
                   jax.ShapeDtypeStruct((lay.t_all, 2 * n_kv), BF16),
                   jax.ShapeDtypeStruct((lay.t_all, 2 * n_kv), BF16)],
        compiler_params=_params(1),
        name="qkv_project",
    )(x, mod, g.reshape(1, d), w.astype(BF16), cos, sina, sinb, pair(q_gain), pair(k_gain), seg_mean)


def _rope_tables(lay):
    rows = lay.seq // 64
    grid_w = 64
    row = jnp.repeat(jnp.arange(rows, dtype=F32), grid_w)
    col = jnp.tile(jnp.arange(grid_w, dtype=F32), rows)
    axis_dim = HEAD // 2
    inv_freq = ROPE_THETA ** (-jnp.arange(0, axis_dim, 2, dtype=F32) / axis_dim)
    a_row = row[:, None] * inv_freq
    a_col = col[:, None] * inv_freq
    ang = jnp.concatenate([a_row, a_row, a_col, a_col], axis=-1)
    cos = jnp.concatenate([jnp.ones((lay.n_ctx, HEAD), F32), jnp.cos(ang)], axis=0)
    sin = jnp.concatenate([jnp.zeros((lay.n_ctx, HEAD), F32), jnp.sin(ang)], axis=0)
    first = (jnp.arange(HEAD) % 32) < 16
    sina = jnp.where(first, -sin, 0.0)
    sinb = jnp.where(first, 0.0, sin)
    wide = lambda t: jnp.tile(t, (1, LANES // HEAD))
    return wide(cos), wide(sina), wide(sinb)


def _attn_kernel(sink_ref, q_ref, k_ref, v_ref, o_ref, *, n_kv, grp, window, n_ctx, seq):
    tq = ATT_TQ
    j = pl.program_id(1)
    rows = grp * tq
    ctx_blocks = n_ctx // tq
    gs = range(n_kv)
    tile = lambda ref, r0, g: ref[r0, g * LANES:(g + 1) * LANES]
    qg = [jnp.concatenate([tile(q_ref, slice(None), g * grp + u) for u in range(grp)], axis=0) for g in gs]

    def step(carry, ks, vs, keep=None):
        s = [_dot_nt(qg[g], ks[g]) for g in gs]
        if keep is not None:
            s = [jnp.where(keep, s[g], NEG_INF) for g in gs]
        m_new = [jnp.maximum(carry[g][0], jnp.max(s[g], axis=1, keepdims=True)) for g in gs]
        p = [jnp.exp(s[g] - m_new[g]).astype(BF16) for g in gs]
        alpha = [jnp.exp(carry[g][0] - m_new[g]) for g in gs]
        pv = [_dot(p[g], vs[g]) for g in gs]
        return tuple((m_new[g], alpha[g] * carry[g][1] + pv[g]) for g in gs)

    if window:
        lane = lax.broadcasted_iota(jnp.int32, (rows, LANES), 1)
        acc0 = jnp.where(lane == HEAD, 1.0, 0.0)
        carry = tuple((jnp.concatenate([jnp.full((tq, 1), sink_ref[g * grp + u], F32) for u in range(grp)],
                                       axis=0), acc0) for g in gs)
        tk = WINDOW
        rr = lax.broadcasted_iota(jnp.int32, (rows, tk), 0) & (tq - 1)
        cc = lax.broadcasted_iota(jnp.int32, (rows, tk), 1)
        j_lat = j - ctx_blocks
        n_lat = seq // tq
        never = 4 * tk
        for cb in range(n_ctx // tk):
            r0 = slice(cb * tk, (cb + 1) * tk)
            carry = step(carry, [tile(k_ref, r0, g) for g in gs], [tile(v_ref, r0, g) for g in gs])
        for off in (-1, 0, 1):
            blk = j_lat + off
            valid = jnp.logical_and(j_lat >= 0, jnp.logical_and(blk >= 0, blk < n_lat))
            r0 = pl.ds(pl.multiple_of(n_ctx + jnp.clip(blk, 0, n_lat - 1) * tk, tk), tk)
            bar = jnp.where(valid, 0, never)
            keep = {-1: cc >= rr + bar, 0: cc >= bar, 1: cc + bar <= rr}[off]
            carry = step(carry, [tile(k_ref, r0, g) for g in gs], [tile(v_ref, r0, g) for g in gs], keep)
    else:
        tk = min(ATT_TK, seq)
        assert seq % tk == 0
        carry = tuple((jnp.full((rows, 1), NEG_INF, F32), jnp.zeros((rows, LANES), F32)) for g in gs)
        ctx_rows = slice(0, n_ctx)
        carry = step(carry, [tile(k_ref, ctx_rows, g) for g in gs], [tile(v_ref, ctx_rows, g) for g in gs])
        n_tiles = jnp.where(j < ctx_blocks, 0, seq // tk)

        def body(t, c):
            r0 = pl.ds(pl.multiple_of(n_ctx + t * tk, math.gcd(n_ctx, tk)), tk)
            return step(c, [tile(k_ref, r0, g) for g in gs], [tile(v_ref, r0, g) for g in gs])

        carry = lax.fori_loop(0, n_tiles, body, carry)
    for g in gs:
        acc = carry[g][1]
        out = acc[:, :HEAD] / acc[:, HEAD:HEAD + 1]
        for u in range(grp):
            hd = g * grp + u
            o_ref[:, hd * HEAD:(hd + 1) * HEAD] = out[u * tq:(u + 1) * tq].astype(BF16)


def _attention(lay, q, k, v, sink, *, n_kv_heads, window):
    d = q.shape[1] // 2
    n_kv = n_kv_heads * LANES
    per_q = lay.per_batch // ATT_TQ
    kv_spec = pl.BlockSpec((lay.per_batch, n_kv), lambda b, j, *_: (b, 0), pipeline_mode=pl.Buffered(1))
    q_spec = pl.BlockSpec((ATT_TQ, 2 * d), lambda b, j, *_: (b * per_q + j, 0))
    o_spec = pl.BlockSpec((ATT_TQ, d), lambda b, j, *_: (b * per_q + j, 0))
    return pl.pallas_call(
        functools.partial(_attn_kernel, n_kv=n_kv_heads, grp=d // HEAD // n_kv_heads, window=window,
                          n_ctx=lay.n_ctx, seq=lay.seq),
        grid_spec=pltpu.PrefetchScalarGridSpec(
            num_scalar_prefetch=1, grid=(lay.n_batch, per_q),
            in_specs=[q_spec, kv_spec, kv_spec], out_specs=o_spec),
        out_shape=jax.ShapeDtypeStruct((lay.t_all, d), BF16),
        compiler_params=_params(2),
        name="window_attention" if window else "dense_attention",
    )(sink, q, k, v)


def _route(logits):
    lane = lax.broadcasted_iota(jnp.int32, logits.shape, 1)
    far = 4 * LANES
    is_g = lane < N_GROUPS
    gmax = jnp.max(jnp.where(is_g, logits, NEG_INF), axis=1, keepdims=True)
    gidx = jnp.min(jnp.where(jnp.logical_and(is_g, logits == gmax), lane, far), axis=1, keepdims=True)
    gsum = jnp.sum(jnp.where(is_g, jnp.exp(logits - gmax), 0.0), axis=1, keepdims=True)
    g_w = 1.0 / gsum
    is_e = jnp.logical_and(lane >= N_GROUPS, ((lane - N_GROUPS) >> 3) == gidx)
    emax = jnp.max(jnp.where(is_e, logits, NEG_INF), axis=1, keepdims=True)
    ee = jnp.where(is_e, jnp.exp(logits - emax), 0.0)
    p = ee / jnp.sum(ee, axis=1, keepdims=True)
    p_in = jnp.where(is_e, p, -1.0)
    p1 = jnp.max(p_in, axis=1, keepdims=True)
    i1 = jnp.min(jnp.where(p_in == p1, lane, far), axis=1, keepdims=True)
    p_rest = jnp.where(lane == i1, -1.0, p_in)
    p2 = jnp.max(p_rest, axis=1, keepdims=True)
    i2 = jnp.min(jnp.where(p_rest == p2, lane, far), axis=1, keepdims=True)
    tot = p1 + p2
    w1 = g_w * p1 / tot
    w2 = g_w * p2 / tot
    e1 = (i1 - N_GROUPS).astype(F32)
    e2 = (i2 - N_GROUPS).astype(F32)
    picks = (jnp.where(lane == i1, 1.0, 0.0), jnp.where(lane == i2, 1.0, 0.0))
    return (e1, e2, w1, w2), picks


def _post_kernel(o_ref, wo_ref, x_ref, mod_ref, g2_ref, wr_hi_ref, wr_lo_ref, br_ref,
                 xn_ref, h2_ref, route_ref, count_ref, seen_ref):
    @pl.when(pl.program_id(0) == 0)
    def _():
        seen_ref[...] = jnp.zeros_like(seen_ref)

    y = _dot(o_ref[...], wo_ref[...])
    xn = x_ref[...] + mod_ref[0, 2:3] * y
    xn_ref[...] = xn
    h2 = _modulate(xn, g2_ref[...], mod_ref[0, 3:4], mod_ref[0, 4:5])
    h2_ref[...] = h2
    h_hi = _bf(h2)
    h_lo = _bf(h2 - h_hi.astype(F32))
    logits = _dot(h_hi, wr_hi_ref[...]) + (_dot(h_hi, wr_lo_ref[...]) + _dot(h_lo, wr_hi_ref[...]))
    cols, (pick1, pick2) = _route(logits + br_ref[...])
    both = pick1 + pick2
    row = lax.broadcasted_iota(jnp.int32, (TM, TM), 0)
    col = lax.broadcasted_iota(jnp.int32, (TM, TM), 1)
    earlier = jnp.where(col < row, 1.0, 0.0).astype(BF16)
    before = _dot(earlier, both.astype(BF16)) + seen_ref[0:1]
    ranks = (jnp.sum(before * pick1, axis=1, keepdims=True), jnp.sum(before * pick2, axis=1, keepdims=True))
    seen = seen_ref[0:1] + jnp.sum(both, axis=0, keepdims=True)
    seen_ref[...] = jnp.broadcast_to(seen, seen_ref.shape)
    count_ref[...] = jnp.broadcast_to(seen, count_ref.shape)
    lane = lax.broadcasted_iota(jnp.int32, (TM, LANES), 1)
    out = jnp.zeros((TM, LANES), F32)
    for i, c in enumerate(cols + ranks):
        out = jnp.where(lane == i, c, out)
    route_ref[...] = out


def _post_mixer(lay, o, w_o, x, mod, g2, w_route, b_route):
    d = x.shape[1]
    w_hi = w_route.astype(BF16)
    return pl.pallas_call(
        _post_kernel,
        grid=(lay.n_blocks,),
        in_specs=[lay.row_spec(d), _full((d, d)), lay.row_spec(d), lay.mod_spec(d), _full((1, d)),
                  _full((d, LANES)), _full((d, LANES)), _full((1, LANES))],
        out_specs=[lay.row_spec(d), lay.row_spec(d), lay.row_spec(LANES), _full((8, LANES))],
        out_shape=[jax.ShapeDtypeStruct((lay.t_all, d), F32), jax.ShapeDtypeStruct((lay.t_all, d), F32),
                   jax.ShapeDtypeStruct((lay.t_all, LANES), F32), jax.ShapeDtypeStruct((8, LANES), F32)],
        scratch_shapes=[pltpu.VMEM((8, LANES), F32)],
        compiler_params=_params(1),
        name="post_mixer",
    )(o, w_o.astype(BF16), x, mod, g2.reshape(1, d), w_hi, (w_route - w_hi.astype(F32)).astype(BF16), b_route)


def _dispatch_kernel(dest_ref, h_ref, xs_in, xs_hbm, sem):
    del xs_in

    def row_copy(r, k):
        return pltpu.make_async_copy(h_ref.at[pl.ds(r, 1)], xs_hbm.at[pl.ds(dest_ref[0, k, r], 1)], sem)

    def start(r, carry):
        row_copy(r, 0).start()
        row_copy(r, 1).start()
        return carry

    def wait(r, carry):
        row_copy(r, 0).wait()
        row_copy(r, 1).wait()
        return carry

    lax.fori_loop(0, TM, start, 0, unroll=8)
    lax.fori_loop(0, TM, wait, 0, unroll=8)


def _dispatch(lay, h2, dest_blocks, slots):
    n_rows, d = slots.shape
    return pl.pallas_call(
        _dispatch_kernel,
        grid=(lay.n_blocks,),
        in_specs=[pl.BlockSpec((1, 2, TM), lambda j: (j, 0, 0), memory_space=pltpu.SMEM), lay.row_spec(d),
                  pl.BlockSpec(memory_space=pl.ANY)],
        out_specs=pl.BlockSpec(memory_space=pl.ANY),
        out_shape=jax.ShapeDtypeStruct((n_rows, d), F32),
        scratch_shapes=[pltpu.SemaphoreType.DMA(())],
        input_output_aliases={2: 0},
        compiler_params=_params(1),
        name="moe_dispatch",
    )(dest_blocks, h2, slots)


def _expert_kernel(blk_ref, used_ref, x_ref, w1_ref, w3_ref, w2_ref, out_ref):
    del blk_ref
    i = pl.program_id(0)

    @pl.when(i < used_ref[0])
    def _():
        xb = x_ref[...].astype(BF16)
        a = _dot(xb, w1_ref[0].astype(BF16))
        hid = (a * _sigmoid(a)) * _dot(xb, w3_ref[0].astype(BF16))
        out_ref[...] = _dot(hid.astype(BF16), w2_ref[0].astype(BF16))

    @pl.when(i >= used_ref[0])
    def _():
        out_ref[...] = jnp.zeros_like(out_ref)


def _expert_blocks(xs, blk_expert, n_used, w1, w3, w2, layer):
    n_rows, d = xs.shape
    n_blocks = n_rows // MOE_ROWS
    de = w1.shape[2]
    base = layer * N_EXPERTS
    return pl.pallas_call(
        _expert_kernel,
        grid_spec=pltpu.PrefetchScalarGridSpec(
            num_scalar_prefetch=2, grid=(n_blocks,),
            in_specs=[pl.BlockSpec((MOE_ROWS, d), lambda i, blk, used: (i, 0)),
                      pl.BlockSpec((1, d, de), lambda i, blk, used: (base + blk[i], 0, 0)),
                      pl.BlockSpec((1, d, de), lambda i, blk, used: (base + blk[i], 0, 0)),
                      pl.BlockSpec((1, de, d), lambda i, blk, used: (base + blk[i], 0, 0))],
            out_specs=pl.BlockSpec((MOE_ROWS, d), lambda i, blk, used: (i, 0))),
        out_shape=jax.ShapeDtypeStruct((n_rows, d), F32),
        compiler_params=_params(1),
        name="expert_blocks",
    )(blk_expert, n_used, xs, w1, w3, w2)


def _combine_kernel(pos_ref, pos_next_ref, ys_hbm, x_ref, mod_ref, route_ref, o_ref, buf, sem):
    j = pl.program_id(0)
    last = pl.num_programs(0) - 1
    slot = j % 2

    def row_copy(p_ref, r, k, s):
        return pltpu.make_async_copy(ys_hbm.at[pl.ds(p_ref[0, k, r], 1)], buf.at[s, k, pl.ds(r, 1)], sem.at[s])

    def issue(p_ref, s):
        def start(r, carry):
            row_copy(p_ref, r, 0, s).start()
            row_copy(p_ref, r, 1, s).start()
            return carry
        lax.fori_loop(0, TM, start, 0, unroll=8)

    @pl.when(j == 0)
    def _():
        issue(pos_ref, 0)

    @pl.when(j < last)
    def _():
        issue(pos_next_ref, 1 - slot)

    def wait(r, carry):
        row_copy(pos_ref, r, 0, slot).wait()
        row_copy(pos_ref, r, 1, slot).wait()
        return carry

    lax.fori_loop(0, TM, wait, 0, unroll=8)
    route = route_ref[...]
    f = buf[slot, 0] * route[:, 2:3] + buf[slot, 1] * route[:, 3:4]
    o_ref[...] = x_ref[...] + mod_ref[0, 5:6] * f


def _combine(lay, ys, pos, x, mod, route):
    d = x.shape[1]
    pos_spec = lambda idx: pl.BlockSpec((1, 2, TM), lambda j: (idx(j), 0, 0), memory_space=pltpu.SMEM)
    return pl.pallas_call(
        _combine_kernel,
        grid=(lay.n_blocks,),
        in_specs=[pos_spec(lambda j: j), pos_spec(lambda j: jnp.minimum(j + 1, lay.n_blocks - 1)),
                  pl.BlockSpec(memory_space=pl.ANY), lay.row_spec(d), lay.mod_spec(d), lay.row_spec(LANES)],
        out_specs=lay.row_spec(d),
        out_shape=jax.ShapeDtypeStruct((lay.t_all, d), F32),
        scratch_shapes=[pltpu.VMEM((2, 2, TM, d), F32), pltpu.SemaphoreType.DMA((2,))],
        compiler_params=_params(1),
        name="moe_combine",
    )(pos, pos, ys, x, mod, route)


def _dispatch_plan(route, counts):
    n_tok = route.shape[0]
    n_blocks = -(-(n_tok * 2) // MOE_ROWS) + N_EXPERTS
    counts = counts[0, N_GROUPS:N_GROUPS + N_EXPERTS].astype(jnp.int32)
    padded = (counts + MOE_ROWS - 1) // MOE_ROWS * MOE_ROWS
    p_end = jnp.cumsum(padded)
    first_slot = p_end - padded
    expert = route[:, 0:2].astype(jnp.int32)
    onehot = expert[:, :, None] == jnp.arange(N_EXPERTS, dtype=jnp.int32)
    dest = jnp.sum(jnp.where(onehot, first_slot, 0), axis=-1) + route[:, 4:6].astype(jnp.int32)
    block_start = jnp.arange(n_blocks, dtype=jnp.int32) * MOE_ROWS
    blk_expert = jnp.minimum(jnp.sum(p_end[None, :] <= block_start[:, None], axis=1), N_EXPERTS - 1).astype(jnp.int32)
    n_used = (p_end[-1:] // MOE_ROWS).astype(jnp.int32)
    return dest, blk_expert, n_used, n_blocks


def _moe(lay, xn, h2, route, counts, mod, w1, w3, w2, layer, slots):
    dest, blk_expert, n_used, n_blocks = _dispatch_plan(route, counts)
    dest_blocks = dest.reshape(lay.n_blocks, TM, 2).transpose(0, 2, 1)
    if slots is None:
        slots = jnp.zeros((n_blocks * MOE_ROWS, h2.shape[1]), F32)
    xs = _dispatch(lay, h2, dest_blocks, slots)
    ys = _expert_blocks(xs, blk_expert, n_used, w1, w3, w2, layer)
    return _combine(lay, ys, dest_blocks, xn, mod, route), xs


def _rwkv_feat_kernel(x_ref, xp_ref, xq_ref, mod_ref, g_ref, mu_ref, wr_ref, wk_ref, wv_ref, w0_ref, w1_ref, w2_ref,
                      a0_ref, a1_ref, a2_ref, g1_ref, g2_ref, kk_ref, ka_ref, seg_ref,
                      r_ref, v_ref, kn_ref, gate_ref, lw_ref, kd_ref, b_ref, *, blocks_per_batch):
    jb = pl.program_id(0) % blocks_per_batch
    shift, scale, g = mod_ref[0, 0:1], mod_ref[0, 1:2], g_ref[...]
    h = _modulate(x_ref[...], g, shift, scale)
    has_prev = jnp.where(jb > 1, 1.0, 0.0)
    has_next = jnp.where(jnp.logical_and(jb > 0, jb < blocks_per_batch - 1), 1.0, 0.0)
    h_before = _modulate(xp_ref[...], g, shift, scale)[7:8] * has_prev
    h_after = _modulate(xq_ref[...], g, shift, scale)[0:1] * has_next
    rows = lax.broadcasted_iota(jnp.int32, h.shape, 0)
    prev = jnp.where(rows == 0, h_before, pltpu.roll(h, 1, 0))
    nxt = jnp.where(rows == TM - 1, h_after, pltpu.roll(h, TM - 1, 0))
    xx = 0.5 * (prev + nxt) - h
    mix = lambda i: (h + xx * mu_ref[i:i + 1]).astype(BF16)
    xr, xw, xk, xv, xa, xg = [mix(i) for i in range(6)]
    r = _dot(xr, wr_ref[...])
    k = _dot(xk, wk_ref[...])
    r_ref[...] = r
    v_ref[...] = _dot(xv, wv_ref[...])
    gate_ref[...] = _dot(_sigmoid(_dot(xg, g1_ref[...])).astype(BF16), g2_ref[...])
    kk = k * kk_ref[...]
    norm = jnp.sqrt(_seg_apply(kk * kk, seg_ref[...]))
    kk = kk / jnp.maximum(norm, 1e-12)
    kn_ref[...] = kk
    for n in range(2):
        w_log = w0_ref[n:n + 1] + _dot(jnp.tanh(_dot(xw, w1_ref[n])).astype(BF16), w2_ref[n])
        z = -w_log
        softplus = jnp.maximum(z, 0.0) + jnp.log(1.0 + jnp.exp(-jnp.abs(z)))
        lw_ref[n] = -jnp.exp(-softplus - 0.5)
        iclr = _sigmoid(a0_ref[n:n + 1] + _dot(_dot(xa, a1_ref[n]).astype(BF16), a2_ref[n]))
        kd_ref[n] = k * (1.0 + (iclr - 1.0) * ka_ref[...])
        b_ref[n] = kk * iclr


def _rwkv_features(lay, x, mod, g, p, seg_sum):
    (mu, w_r, w_k, w_v, w0, w1, w2, a0, a1, a2, g1, g2, k_k, k_a) = p
    d = x.shape[1]
    row = lay.row_spec(d)
    sub = TM // 8
    last = lay.t_all // 8 - 1
    before = pl.BlockSpec((8, d), lambda j: (jnp.maximum(j * sub - 1, 0), 0))
    after = pl.BlockSpec((8, d), lambda j: (jnp.minimum((j + 1) * sub, last), 0))
    both = pl.BlockSpec((2, TM, d), lambda j: (0, j, 0))
    bf = lambda t: t.astype(BF16)
    args = (x, x, x, mod, g.reshape(1, d), mu, bf(w_r), bf(w_k), bf(w_v), w0, bf(w1), bf(w2), a0, bf(a1), bf(a2),
            bf(g1), bf(g2), k_k.reshape(1, d), k_a.reshape(1, d), seg_sum)
    in_specs = [row, before, after, lay.mod_spec(d)] + [_full(a.shape) for a in args[4:]]
    shape = jax.ShapeDtypeStruct((lay.t_all, d), F32)
    shape2 = jax.ShapeDtypeStruct((2, lay.t_all, d), F32)
    return pl.pallas_call(
        functools.partial(_rwkv_feat_kernel, blocks_per_batch=lay.blocks_per_batch),
        grid=(lay.n_blocks,),
        in_specs=in_specs,
        out_specs=[row, row, row, row, both, both, both],
        out_shape=[shape, shape, shape, shape, shape2, shape2, shape2],
        compiler_params=_params(1),
        name="rwkv_features",
    )(*args)


def _scan_operands(sign, r_ref, v_ref, kk_ref, lw_ref, kd_ref, b_ref, n_heads):
    c = SCAN_CHUNK
    row = lax.broadcasted_iota(jnp.int32, (c, c), 0)
    col = lax.broadcasted_iota(jnp.int32, (c, c), 1)
    incl = (row - col) * sign >= 0
    lw = lw_ref[0]
    tri = jnp.where(incl, 1.0, 0.0).astype(BF16)
    lw_hi = _bf(lw)
    lw_lo = _bf(lw - lw_hi.astype(F32))
    lg = _dot(jnp.concatenate([tri, tri], axis=1), jnp.concatenate([lw_hi, lw_lo], axis=0))
    lg_tot = jnp.sum(lw, axis=0, keepdims=True)
    g_inv = jnp.exp(-lg)
    g_rem = jnp.exp(lg_tot - lg)
    g_tot = jnp.exp(lg_tot)
    v_all = v_ref[...]
    even = (lax.broadcasted_iota(jnp.int32, lw.shape, 1) & HEAD) == 0
    halves = lambda t: (_bf(jnp.where(even, t, 0.0)), _bf(jnp.where(even, 0.0, t)))
    ar_all = _bf(jnp.concatenate([-kk_ref[...] * jnp.exp(lg - lw), r_ref[...] * jnp.exp(lg)], axis=0))
    b_t, k_t = halves(b_ref[0] * g_inv), halves(kd_ref[0] * g_inv)
    b_h, k_h = halves(b_ref[0] * g_rem), halves(kd_ref[0] * g_rem)

    lag2 = (lax.broadcasted_iota(jnp.int32, (c, LANES), 0)
            - (lax.broadcasted_iota(jnp.int32, (c, LANES), 1) & (HEAD - 1))) * sign
    hs = range(n_heads)
    pair = [slice(h // 2 * LANES, (h // 2 + 1) * LANES) for h in hs]
    return dict(
        strict2=[lag2 > 0] * n_heads,
        incl2=[lag2 >= 0] * n_heads,
        ar=[ar_all[:, pair[h]] for h in hs],
        bk=[jnp.concatenate([b_t[h % 2][:, pair[h]], k_t[h % 2][:, pair[h]], b_t[h % 2][:, pair[h]],
                             b_t[h % 2][:, pair[h]]], axis=0) for h in hs],
        bkh=[jnp.concatenate([b_h[h % 2][:, pair[h]], k_h[h % 2][:, pair[h]]], axis=0) for h in hs],
        v_h=[v_all[:, h * HEAD:(h + 1) * HEAD] for h in hs],
        g_tot=[g_tot[:, pair[h]] for h in hs])


def _scan_kernel(rf_ref, vf_ref, kkf_ref, lwf_ref, kdf_ref, bf_ref, rb_ref, vb_ref, kkb_ref, lwb_ref, kdb_ref, bb_ref,
                 yf_ref, yb_ref, s_ref, *, n_heads):
    c = SCAN_CHUNK

    @pl.when(pl.program_id(1) == 0)
    def _():
        s_ref[...] = jnp.zeros_like(s_ref)

    fwd = _scan_operands(1, rf_ref, vf_ref, kkf_ref, lwf_ref, kdf_ref, bf_ref, n_heads)
    bwd = _scan_operands(-1, rb_ref, vb_ref, kkb_ref, lwb_ref, kdb_ref, bb_ref, n_heads)
    op = {key: fwd[key] + bwd[key] for key in fwd}
    hs = range(2 * n_heads)
    zeros_v = jnp.zeros((c, HEAD), BF16)
    zeros_l = jnp.zeros((c, LANES), BF16)
    s0 = [s_ref[h] for h in hs]
    p = [_dot_nt(op["ar"][h], op["bk"][h]) for h in hs]
    ars = [_dot_nt(op["ar"][h], _bf(s0[h])) for h in hs]
    x = [ars[h][:c] + _dot(_bf(jnp.where(op["strict2"][h], p[h][:c, :LANES], 0.0)),
                           jnp.concatenate([zeros_v, _bf(op["v_h"][h])], axis=0)) for h in hs]
    pw = [_bf(jnp.where(op["strict2"][h], p[h][:c, LANES:], 0.0)) for h in hs]
    x = [x[h] + _apply_split(pw[h], x[h]) for h in hs]
    for _ in range(5):
        pw = [_bf(_dot(pw[h], jnp.concatenate([pw[h], zeros_l], axis=0))) for h in hs]
        x = [x[h] + _apply_split(pw[h], x[h]) for h in hs]
    uv = [_bf(jnp.concatenate([x[h], op["v_h"][h]], axis=0)) for h in hs]
    y = [ars[h][c:] + _dot(_bf(jnp.where(op["incl2"][h], p[h][c:, :LANES], 0.0)), uv[h]) for h in hs]
    yf_ref[...] = jnp.concatenate(y[:n_heads], axis=1)
    yb_ref[...] = jnp.concatenate(y[n_heads:], axis=1)
    states = [s0[h] * op["g_tot"][h] + _dot_tn(uv[h], op["bkh"][h]) for h in hs]
    for h in hs:
        s_ref[h] = states[h]


def _bf(t):
    return t.astype(BF16)


def _apply_split(l_dup, x):
    x_hi = _bf(x)
    x_lo = _bf(x - x_hi.astype(F32))
    return _dot(l_dup, jnp.concatenate([x_hi, x_lo], axis=0))


def _rwkv_scan(r, v, kk, lw, kd, b, *, n_batch, n_ctx):
    t_all, d = r.shape
    c = SCAN_CHUNK
    per_batch = t_all // n_batch // c
    ctx_chunks = n_ctx // c
    n_heads = d // HEAD

    fwd_chunk = lambda bi, i: bi * per_batch + i
    bwd_chunk = lambda bi, i: bi * per_batch + jnp.where(i < ctx_chunks, ctx_chunks - 1 - i,
                                                         per_batch - 1 - (i - ctx_chunks))
    specs = []
    for dr, chunk in ((0, fwd_chunk), (1, bwd_chunk)):
        shared = pl.BlockSpec((c, d), lambda bi, i, chunk=chunk: (chunk(bi, i), 0))
        per_dir = pl.BlockSpec((1, c, d), lambda bi, i, chunk=chunk, dr=dr: (dr, chunk(bi, i), 0))
        specs.append((shared, per_dir))
    in_specs = [sp for shared, per_dir in specs for sp in (shared, shared, shared, per_dir, per_dir, per_dir)]
    out = jax.ShapeDtypeStruct((t_all, d), F32)
    return pl.pallas_call(
        functools.partial(_scan_kernel, n_heads=n_heads),
        grid=(n_batch, per_batch),
        in_specs=in_specs,
        out_specs=[specs[0][0], specs[1][0]],
        out_shape=[out, out],
        scratch_shapes=[pltpu.VMEM((2 * n_heads, HEAD, LANES), F32)],
        compiler_params=_params(2),
        name="rwkv_scan",
    )(r, v, kk, lw, kd, b, r, v, kk, lw, kd, b)


def _rwkv_out_kernel(yf_ref, yb_ref, r_ref, v_ref, gate_ref, kd_ref, rk_ref, lnw_ref, lnb_ref, seg_ref, o_ref):
    seg = seg_ref[...]
    y = yf_ref[...] + yb_ref[...]
    mean = _seg_apply(y, seg)
    cen = y - mean
    var = _seg_apply(cen * cen, seg)
    yn = cen * lax.rsqrt(var + LN_X_EPS) * lnw_ref[...] + lnb_ref[...]
    bonus = _seg_apply(r_ref[...] * (kd_ref[0] + kd_ref[1]) * rk_ref[...], seg) * float(HEAD) * v_ref[...]
    o_ref[...] = ((yn + bonus) * gate_ref[...]).astype(BF16)


def _rwkv_out(lay, y_f, y_b, r, v, gate, kd, r_k, ln_w, ln_b, seg_mean):
    d = r.shape[1]
    row = lay.row_spec(d)
    both = pl.BlockSpec((2, TM, d), lambda j: (0, j, 0))
    return pl.pallas_call(
        _rwkv_out_kernel,
        grid=(lay.n_blocks,),
        in_specs=[row, row, row, row, row, both, _full((1, d)), _full((1, d)), _full((1, d)), _full((LANES, LANES))],
        out_specs=row,
        out_shape=jax.ShapeDtypeStruct((lay.t_all, d), BF16),
        compiler_params=_params(1),
        name="rwkv_out",
    )(y_f, y_b, r, v, gate, kd, r_k.reshape(1, d), ln_w.reshape(1, d), ln_b.reshape(1, d), seg_mean)


def _final_kernel(x_ref, g_ref, o_ref):
    x = x_ref[...]
    o_ref[0] = x * lax.rsqrt(jnp.mean(x * x, axis=-1, keepdims=True) + NORM_EPS) * g_ref[...]


def _final_norm(lay, x, g):
    d = x.shape[1]
    lat_blocks = lay.seq // TM
    return pl.pallas_call(
        _final_kernel,
        grid=(lay.n_batch, lat_blocks),
        in_specs=[pl.BlockSpec((TM, d), lambda b, j: (b * lay.blocks_per_batch + 1 + j, 0)), _full((1, d))],
        out_specs=pl.BlockSpec((1, TM, d), lambda b, j: (b, j, 0)),
        out_shape=jax.ShapeDtypeStruct((lay.n_batch, lay.seq, d), F32),
        compiler_params=_params(2),
        name="final_norm",
    )(x, g.reshape(1, d))


def kernel(x, c, ctx, c_ctx, mod_w, mod_b, norm1_g, norm2_g, a_w_qkv, a_w_o, a_q_gain, a_k_gain, b_mu, b_w_r, b_w_k, b_w_v, b_w_o, b_decay_w0, b_decay_w1, b_decay_w2, b_iclr_a0, b_iclr_a1, b_iclr_a2, b_gate_g1, b_gate_g2, b_k_k, b_k_a, b_r_k, b_ln_w, b_ln_b, c_w_qkv, c_w_o, c_sink, moe_w_group, moe_b_group, moe_w_expert, moe_b_expert, moe_w1, moe_w3, moe_w2, final_g):
    n_batch, seq, d = x.shape
    n_ctx = ctx.shape[1]
    depth = mod_w.shape[0]
    lay = _Layout(n_batch, n_ctx, seq)

    cond = jnp.zeros((8, d), F32).at[:n_batch].set(c).at[n_batch].set(c_ctx)
    mods = _mod_vectors(cond, mod_w, mod_b)[:, :n_batch + 1].reshape(depth, n_batch + 1, 6, d)
    xs = jnp.concatenate([ctx, x], axis=1).reshape(lay.t_all, d)

    blockdiag = (jnp.arange(LANES)[:, None] // HEAD == jnp.arange(LANES)[None, :] // HEAD).astype(F32)
    seg_sum, seg_mean = blockdiag, blockdiag / HEAD
    rope = _rope_tables(lay)
    no_gain = jnp.ones((HEAD,), F32)
    stacked = lambda w: w.reshape((-1,) + w.shape[2:])
    w1_all, w3_all, w2_all = stacked(moe_w1), stacked(moe_w3), stacked(moe_w2)
    no_sink = jnp.zeros((d // HEAD,), F32)

    slots = None
    for l in range(depth):
        kind, j = l % 3, l // 3
        mod = mods[l]
        if kind == 0:
            q, k, v = _qkv_project(lay, xs, mod, norm1_g[l], a_w_qkv[j], rope, a_q_gain[j], a_k_gain[j], seg_mean,
                                   n_kv_heads=4, qk_norm=True)
            o = _attention(lay, q, k, v, no_sink, n_kv_heads=4, window=False)
            w_o = a_w_o[j]
        elif kind == 1:
            p = (b_mu[j], b_w_r[j], b_w_k[j], b_w_v[j], b_decay_w0[j], b_decay_w1[j], b_decay_w2[j],
                 b_iclr_a0[j], b_iclr_a1[j], b_iclr_a2[j], b_gate_g1[j], b_gate_g2[j], b_k_k[j], b_k_a[j])
            r, v, kk, gate, lw, kd, b = _rwkv_features(lay, xs, mod, norm1_g[l], p, seg_sum)
            y_f, y_b = _rwkv_scan(r, v, kk, lw, kd, b, n_batch=n_batch, n_ctx=n_ctx)
            o = _rwkv_out(lay, y_f, y_b, r, v, gate, kd, b_r_k[j], b_ln_w[j], b_ln_b[j], seg_mean)
            w_o = b_w_o[j]
        else:
            q, k, v = _qkv_project(lay, xs, mod, norm1_g[l], c_w_qkv[j], rope, no_gain, no_gain, seg_mean,
                                   n_kv_heads=2, qk_norm=False)
            o = _attention(lay, q, k, v, c_sink[j], n_kv_heads=2, window=True)
            w_o = c_w_o[j]
        w_route = jnp.zeros((d, LANES), F32).at[:, :N_GROUPS].set(moe_w_group[l]) \
            .at[:, N_GROUPS:N_GROUPS + N_EXPERTS].set(moe_w_expert[l])
        b_route = jnp.zeros((1, LANES), F32).at[0, :N_GROUPS].set(moe_b_group[l]) \
            .at[0, N_GROUPS:N_GROUPS + N_EXPERTS].set(moe_b_expert[l])
        xn, h2, route, counts = _post_mixer(lay, o, w_o, xs, mod, norm2_g[l], w_route, b_route)
        xs, slots = _moe(lay, xn, h2, route, counts, mod, w1_all, w3_all, w2_all, l, slots)
    return _final_norm(lay, xs, final_g)
```
